```python
import math
import jax, jax.numpy as jnp
from jax import lax
import numpy as np

D_MODEL = 2048
BATCH = 4
SEQ = 4096
DEPTH = 4

HEAD_DIM = 128
N_QK_HEADS_A = 8
N_V_HEADS_A = 16
QK_WIDTH_A = N_QK_HEADS_A * HEAD_DIM
V_WIDTH_A = N_V_HEADS_A * HEAD_DIM
CONV_A = 4
CHUNK_A = 64
N_GROUPS_B = 8
GROUP_DIM_B = 128
WIDTH_B = N_GROUPS_B * GROUP_DIM_B
CHUNK_B = 128
N_BRANCH = 2
D_FF = 5632
CONV_FFN = 3
PLE_DIM = 256
EPS = 1e-6

SPLIT_SIZES = (QK_WIDTH_A, QK_WIDTH_A, V_WIDTH_A, V_WIDTH_A, N_V_HEADS_A, N_V_HEADS_A,
               WIDTH_B, WIDTH_B, N_BRANCH * D_MODEL)
N_IN = sum(SPLIT_SIZES)

kernel_name = "hybrid_deltanet_sgu_convglu_ple"


def rms_norm(x, gain):
    xf = x.astype(jnp.float32)
    y = xf * lax.rsqrt(jnp.mean(xf * xf, axis=-1, keepdims=True) + EPS)
    return (y * gain.astype(jnp.float32)).astype(x.dtype)


def l2_normalize(x):
    return x * lax.rsqrt(jnp.sum(x * x, axis=-1, keepdims=True) + EPS)


def causal_depthwise_conv(x, w):
    k = w.shape[0]
    return lax.conv_general_dilated(
        x, w[:, None, :].astype(x.dtype), window_strides=(1,), padding=((k - 1, 0),),
        dimension_numbers=("NWC", "WIO", "NWC"), feature_group_count=x.shape[-1])


def gated_delta_rule(q, k, v, beta, g):
    b, s, h, dk = q.shape
    dv = v.shape[-1]
    n = s // CHUNK_A

    def chunks(t):
        t = t.reshape((b, n, CHUNK_A, h) + t.shape[3:])
        return jnp.moveaxis(t, (1, 3), (0, 2))

    qc, kc, vc = chunks(q), chunks(k), chunks(v)
    bc, gc = chunks(beta), chunks(g)
    gam = jnp.cumsum(gc, axis=-1)
    causal = jnp.tril(jnp.ones((CHUNK_A, CHUNK_A), dtype=bool))
    strict = jnp.tril(jnp.ones((CHUNK_A, CHUNK_A), dtype=bool), -1)
    decay = jnp.exp(jnp.where(causal, gam[..., :, None] - gam[..., None, :], -jnp.inf))
    kb = kc * bc[..., None]
    a_mat = jnp.where(strict, jnp.einsum("nbhcd,nbhsd->nbhcs", kb, kc) * decay, 0.0)
    eye = jnp.eye(CHUNK_A, dtype=jnp.float32)
    rhs = jnp.concatenate([vc * bc[..., None], kb * jnp.exp(gam)[..., None]], axis=-1)
    sol = lax.linalg.triangular_solve(a_mat + eye, rhs, left_side=True, lower=True,
                                      unit_diagonal=True)
    u, w = sol[..., :dv], sol[..., dv:]
    attn = jnp.einsum("nbhcd,nbhsd->nbhcs", qc, kc) * decay
    q_dec = qc * jnp.exp(gam)[..., None]
    k_dec = kc * jnp.exp(gam[..., -1:] - gam)[..., None]
    chunk_decay = jnp.exp(gam[..., -1])

    def step(state, inp):
        q_i, k_i, u_i, w_i, a_i, d_i = inp
        v_new = u_i - jnp.einsum("bhcd,bhde->bhce", w_i, state)
        o = jnp.einsum("bhcd,bhde->bhce", q_i, state) + jnp.einsum("bhcs,bhse->bhce", a_i, v_new)
        state = state * d_i[..., None, None] + jnp.einsum("bhcd,bhce->bhde", k_i, v_new)
        return state, o

    s0 = jnp.zeros((b, h, dk, dv), jnp.float32)
    _, o = lax.scan(step, s0, (q_dec, k_dec, u, w, attn, chunk_decay))
    return jnp.moveaxis(o, (0, 2), (1, 3)).reshape(b, s, h, dv)


def delta_mixer(q, k, v, z, beta_logit, a_logit, conv_w, a_log, dt_bias, head_gain):
    b, s, _ = q.shape
    f32 = jnp.float32
    qkv = jax.nn.silu(causal_depthwise_conv(jnp.concatenate([q, k, v], axis=-1), conv_w))
    q, k, v = jnp.split(qkv, [QK_WIDTH_A, 2 * QK_WIDTH_A], axis=-1)
    rep = N_V_HEADS_A // N_QK_HEADS_A
    q = l2_normalize(q.reshape(b, s, N_QK_HEADS_A, HEAD_DIM).astype(f32)) * (HEAD_DIM ** -0.5)
    k = l2_normalize(k.reshape(b, s, N_QK_HEADS_A, HEAD_DIM).astype(f32))
    q = jnp.repeat(q, rep, axis=2)
    k = jnp.repeat(k, rep, axis=2)
    v = v.reshape(b, s, N_V_HEADS_A, HEAD_DIM).astype(f32)
    beta = jax.nn.sigmoid(beta_logit.astype(f32))
    g = -jnp.exp(a_log.astype(f32)) * jax.nn.softplus(a_logit.astype(f32) + dt_bias.astype(f32))
    o = gated_delta_rule(q, k, v, beta, g)
    o = rms_norm(o, head_gain) * jax.nn.silu(z.reshape(b, s, N_V_HEADS_A, HEAD_DIM).astype(f32))
    return o.reshape(b, s, V_WIDTH_A).astype(z.dtype)


def spatial_gating_mixer(u, v, norm_gain, w_s, b_s):
    b, s, _ = u.shape
    n = s // CHUNK_B
    u = jax.nn.gelu(u)
    v = rms_norm(jax.nn.gelu(v), norm_gain)
    causal = jnp.tril(jnp.ones((CHUNK_B, CHUNK_B), dtype=bool))
    w = jnp.where(causal, w_s, 0.0).astype(v.dtype)
    vc = v.reshape(b, n, CHUNK_B, N_GROUPS_B, GROUP_DIM_B)
    mixed = jnp.einsum("gts,bnsgc->bntgc", w, vc) + b_s.T[None, None, :, :, None].astype(v.dtype)
    return u * mixed.reshape(b, s, WIDTH_B)


def conv_glu_ffn(h, w_up, conv_w, conv_b, w_down):
    up = causal_depthwise_conv(h @ w_up, conv_w) + conv_b
    gate, val = jnp.split(up, 2, axis=-1)
    return (jax.nn.silu(gate) * val) @ w_down


def setup_inputs(seed: int = 0) -> dict:
    key = jax.random.key(seed)
    ks = jax.random.split(key, 24)
    f32 = jnp.float32

    def nrm(k, shape, scale):
        return jax.random.normal(k, shape, f32) * scale

    def gain(k, shape):
        return 1.0 + 0.02 * jax.random.normal(k, shape, f32)

    dt = jnp.exp(jax.random.uniform(ks[5], (DEPTH, N_V_HEADS_A), f32, math.log(1e-3), math.log(1e-1)))
    return {
        "x": nrm(ks[0], (BATCH, SEQ, D_MODEL), 1.0),
        "p": nrm(ks[1], (DEPTH, BATCH, SEQ, PLE_DIM), 1.0),
        "norm_mix": gain(ks[2], (DEPTH, D_MODEL)),
        "w_in": nrm(ks[3], (DEPTH, D_MODEL, N_IN), D_MODEL ** -0.5),
        "conv_qkv": nrm(ks[6], (DEPTH, CONV_A, 2 * QK_WIDTH_A + V_WIDTH_A), CONV_A ** -0.5),
        "a_log": jnp.log(jax.random.uniform(ks[4], (DEPTH, N_V_HEADS_A), f32, 1.0, 16.0)),
        "dt_bias": dt + jnp.log(-jnp.expm1(-dt)),
        "head_norm": gain(ks[7], (DEPTH, HEAD_DIM)),
        "sgu_norm": gain(ks[8], (DEPTH, WIDTH_B)),
        "w_spatial": nrm(ks[9], (DEPTH, N_GROUPS_B, CHUNK_B, CHUNK_B), CHUNK_B ** -0.5),
        "b_spatial": 1.0 + nrm(ks[10], (DEPTH, N_GROUPS_B, CHUNK_B), 0.1),
        "w_branch_a": nrm(ks[11], (DEPTH, V_WIDTH_A, D_MODEL), V_WIDTH_A ** -0.5),
        "w_branch_b": nrm(ks[12], (DEPTH, WIDTH_B, D_MODEL), WIDTH_B ** -0.5),
        "w_out": nrm(ks[13], (DEPTH, D_MODEL, D_MODEL), D_MODEL ** -0.5),
        "norm_ffn": gain(ks[14], (DEPTH, D_MODEL)),
        "w_ffn_up": nrm(ks[15], (DEPTH, D_MODEL, 2 * D_FF), D_MODEL ** -0.5),
        "conv_ffn": nrm(ks[16], (DEPTH, CONV_FFN, 2 * D_FF), CONV_FFN ** -0.5),
        "b_conv_ffn": nrm(ks[17], (DEPTH, 2 * D_FF), 0.02),
        "w_ffn_down": nrm(ks[18], (DEPTH, D_FF, D_MODEL), D_FF ** -0.5),
        "norm_ple": gain(ks[19], (DEPTH, D_MODEL)),
        "w_ple_gate": nrm(ks[20], (DEPTH, D_MODEL, D_MODEL), D_MODEL ** -0.5),
        "w_ple_proj": nrm(ks[21], (DEPTH, PLE_DIM, D_MODEL), PLE_DIM ** -0.5),
        "norm_final": gain(ks[22], (D_MODEL,)),
    }


def reference(x, p, norm_mix, w_in, conv_qkv, a_log, dt_bias, head_norm, sgu_norm, w_spatial,
              b_spatial, w_branch_a, w_branch_b, w_out, norm_ffn, w_ffn_up, conv_ffn, b_conv_ffn,
              w_ffn_down, norm_ple, w_ple_gate, w_ple_proj, norm_final):
    split_idx = np.cumsum(SPLIT_SIZES)[:-1].tolist()
    for i in range(DEPTH):
        h = rms_norm(x, norm_mix[i])
        proj = h @ w_in[i]
        q, k, v, z, b_logit, a_logit, u_b, v_b, gates = jnp.split(proj, split_idx, axis=-1)
        y_a = delta_mixer(q, k, v, z, b_logit, a_logit, conv_qkv[i], a_log[i], dt_bias[i],
                          head_norm[i]) @ w_branch_a[i]
        y_b = spatial_gating_mixer(u_b, v_b, sgu_norm[i], w_spatial[i], b_spatial[i]) @ w_branch_b[i]
        g_a, g_b = jnp.split(gates, N_BRANCH, axis=-1)
        merged = jax.nn.sigmoid(g_a) * y_a + jax.nn.sigmoid(g_b) * y_b
        x = x + merged @ w_out[i]
        h = rms_norm(x, norm_ffn[i])
        x = x + conv_glu_ffn(h, w_ffn_up[i], conv_ffn[i], b_conv_ffn[i], w_ffn_down[i])
        h = rms_norm(x, norm_ple[i])
        x = x + jax.nn.sigmoid(h @ w_ple_gate[i]) * (p[i] @ w_ple_proj[i])
    return rms_norm(x, norm_final)
```

```python
import functools

import jax
import jax.numpy as jnp
from jax import lax
from jax.experimental import pallas as pl
from jax.experimental.pallas import tpu as pltpu

F32 = jnp.float32
BF16 = jnp.bfloat16
EPS = 1e-6

HEAD_DIM = 128
N_QK_HEADS = 8
N_V_HEADS = 16
QK_WIDTH = N_QK_HEADS * HEAD_DIM
V_WIDTH = N_V_HEADS * HEAD_DIM
CONV_A = 4
CHUNK_A = 64
N_GROUPS_B = 8
GROUP_DIM_B = 128
WIDTH_B = N_GROUPS_B * GROUP_DIM_B
CHUNK_B = 128
D_FF = 5632
CONV_FFN = 3

COL_Q = 0
COL_K = QK_WIDTH
COL_V = 2 * QK_WIDTH
COL_Z = COL_V + V_WIDTH
COL_NARROW = COL_Z + V_WIDTH
N_NARROW = 2 * N_V_HEADS
COL_U = COL_Z + V_WIDTH
COL_VB = COL_U + WIDTH_B
COL_GA = COL_VB + WIDTH_B
N_MAIN = COL_GA + 2 * 2048
LANES = 128
HALO = 8

V7X_VMEM_LIMIT_BYTES = 52 * 1024 * 1024


def _params(sem):
    return pltpu.CompilerParams(dimension_semantics=sem, vmem_limit_bytes=V7X_VMEM_LIMIT_BYTES)


def _row_tile(s, target):
    t = min(target, s)
    while s % t:
        t //= 2
    return t


def _col_tile(n, target):
    t = min(target, n)
    while n % t or t % LANES:
        t -= LANES
    return t


def _rms_rows_to(x_ref, gain_ref, h_ref, rows=32):
    tm = x_ref.shape[0]

    def body(r, carry):
        sl = pl.ds(pl.multiple_of(r * rows, rows), rows)
        x = x_ref[sl, :]
        ms = jnp.mean(x * x, axis=-1, keepdims=True)
        h_ref[sl, :] = ((x * lax.rsqrt(ms + EPS)) * gain_ref[...]).astype(h_ref.dtype)
        return carry

    lax.fori_loop(0, tm // rows, body, 0)


def _inproj_kernel(x_ref, gain_ref, w_ref, wn_ref, out_ref, nar_ref, h_ref):
    @pl.when(pl.program_id(1) == 0)
    def _():
        _rms_rows_to(x_ref, gain_ref, h_ref)
        nar_ref[...] = jnp.dot(h_ref[...], wn_ref[...], preferred_element_type=F32)

    out_ref[...] = jnp.dot(h_ref[...], w_ref[...], preferred_element_type=F32).astype(out_ref.dtype)


def _inproj(xf, gain, w_main, w_nar, tm, tn):
    m, d = xf.shape
    n = w_main.shape[1]
    return pl.pallas_call(
        _inproj_kernel,
        grid=(m // tm, n // tn),
        in_specs=[
            pl.BlockSpec((tm, d), lambda i, j: (i, 0)),
            pl.BlockSpec((1, d), lambda i, j: (0, 0)),
            pl.BlockSpec((d, tn), lambda i, j: (0, j)),
            pl.BlockSpec((d, LANES), lambda i, j: (0, 0)),
        ],
        out_specs=[
            pl.BlockSpec((tm, tn), lambda i, j: (i, j)),
            pl.BlockSpec((tm, LANES), lambda i, j: (i, 0)),
        ],
        out_shape=[jax.ShapeDtypeStruct((m, n), BF16), jax.ShapeDtypeStruct((m, LANES), F32)],
        scratch_shapes=[pltpu.VMEM((tm, d), BF16)],
        compiler_params=_params(("parallel", "arbitrary")),
        name="inproj",
    )(xf, gain.reshape(1, d), w_main, w_nar)


def _conv_silu(raw, tail, w_ref):
    rows = raw.shape[0]
    ext = jnp.concatenate([tail, raw], axis=0)
    acc = w_ref[CONV_A - 1:CONV_A, :] * raw
    for j in range(CONV_A - 1):
        off = HALO - (CONV_A - 1) + j
        acc = acc + w_ref[j:j + 1, :] * ext[off:off + rows, :]
    return jax.nn.silu(acc), raw[rows - HALO:, :]


def _softplus(x):
    return jnp.maximum(x, 0.0) + jnp.log1p(jnp.exp(-jnp.abs(x)))


def _delta_kernel(q_ref, k_ref, v_ref, z_ref, nar_ref, cq_ref, ck_ref, cv_ref, alog_ref, dtb_ref, hg_ref,
                  o_ref, s_ref):
    c = CHUNK_A
    seq = q_ref.shape[0]
    hq = pl.program_id(1)
    s_ref[...] = jnp.zeros_like(s_ref)

    row = lax.broadcasted_iota(jnp.int32, (2 * c, 2 * c), 0)
    col = lax.broadcasted_iota(jnp.int32, (2 * c, 2 * c), 1)
    same_head = (row < c) == (col < c)
    causal = same_head & (row >= col)
    strict = same_head & (row > col)
    lane_lo = col < c
    beta_lane = jnp.where(row < c, 2 * hq, 2 * hq + 1)
    gate_lane = beta_lane + N_V_HEADS
    gate_row = jnp.where(lane_lo, 2 * hq, 2 * hq + 1) + N_V_HEADS
    row_c = lax.broadcasted_iota(jnp.int32, (c, LANES), 0)
    neg_exp_alog = -jnp.exp(alog_ref[...])
    q_scale = HEAD_DIM ** -0.5

    def chunk(ci, carry):
        tq, tk, tv = carry
        rows = pl.ds(pl.multiple_of(ci * c, c), c)
        qc, tq = _conv_silu(q_ref[rows, :].astype(F32), tq, cq_ref)
        kc, tk = _conv_silu(k_ref[rows, :].astype(F32), tk, ck_ref)
        vc, tv = _conv_silu(v_ref[rows, :].astype(F32), tv, cv_ref)
        qn = qc * (lax.rsqrt(jnp.sum(qc * qc, axis=-1, keepdims=True) + EPS) * q_scale)
        kn = kc * lax.rsqrt(jnp.sum(kc * kc, axis=-1, keepdims=True) + EPS)

        nar = nar_ref[rows, :]
        sig = jax.nn.sigmoid(nar)
        gam = neg_exp_alog * _softplus(nar + dtb_ref[...])
        for sh in (1, 2, 4, 8, 16, 32):
            gam = gam + jnp.where(row_c >= sh, pltpu.roll(gam, sh, axis=0), 0.0)

        sig2 = jnp.concatenate([sig, sig], axis=0)
        gam2 = jnp.concatenate([gam, gam], axis=0)
        beta_c = jnp.sum(jnp.where(col == beta_lane, sig2, 0.0), axis=1, keepdims=True)
        gam_c = jnp.sum(jnp.where(col == gate_lane, gam2, 0.0), axis=1, keepdims=True)
        gam_r = jnp.sum(jnp.where(row == gate_row, gam2.T, 0.0), axis=0, keepdims=True)
        gam_cb = jnp.broadcast_to(gam_c, (2 * c, 2 * c))
        diff = gam_cb - gam_r
        decay = jnp.where(causal, jnp.exp(jnp.where(causal, diff, 0.0)), 0.0)
        egam_c = jnp.exp(gam_c)
        glast_r = jnp.where(lane_lo[0:1, :], gam_cb[c - 1:c, :], gam_cb[2 * c - 1:2 * c, :])
        kdec_scale = jnp.exp(glast_r - gam_r)

        kn2 = jnp.concatenate([kn, kn], axis=0)
        kt2 = kn2.T
        kt2_b = kt2.astype(BF16)
        r = jnp.dot(jnp.concatenate([qn, kn], axis=0).astype(BF16), kt2_b, preferred_element_type=F32)
        qk2 = jnp.concatenate([r[:c], r[:c]], axis=0)
        kk2 = jnp.concatenate([r[c:], r[c:]], axis=0)
        a_mat = jnp.where(strict, kk2 * beta_c * decay, 0.0)
        attn = qk2 * decay

        v2 = jnp.concatenate([vc[:, :HEAD_DIM], vc[:, HEAD_DIM:]], axis=0)
        x = jnp.concatenate([v2 * beta_c, kn2 * (beta_c * egam_c)], axis=1)
        pw = (-a_mat).astype(BF16)
        x = x + jnp.dot(pw, x.astype(BF16), preferred_element_type=F32)
        for _ in range(5):
            pw = jnp.dot(pw, pw, preferred_element_type=F32).astype(BF16)
            x = x + jnp.dot(pw, x.astype(BF16), preferred_element_type=F32)
        u = x[:, :HEAD_DIM]
        w = x[:, HEAD_DIM:]
        qd2 = jnp.concatenate([qn, qn], axis=0) * egam_c

        vnew, o_state = [], []
        for h in range(2):
            hs = slice(h * c, (h + 1) * c)
            res = jnp.dot(jnp.concatenate([w[hs], qd2[hs]], axis=0).astype(BF16), s_ref[h].astype(BF16),
                          preferred_element_type=F32)
            vnew.append(u[hs] - res[:c])
            o_state.append(res[c:])
        vnew_b = jnp.concatenate(vnew, axis=0).astype(BF16)
        o2 = jnp.concatenate(o_state, axis=0) + jnp.dot(attn.astype(BF16), vnew_b, preferred_element_type=F32)

        kdec = kt2 * kdec_scale
        for h in range(2):
            keep = lane_lo if h == 0 else jnp.logical_not(lane_lo)
            cd = jnp.exp(gam_cb[(h + 1) * c - 1:(h + 1) * c, :])
            upd = jnp.dot(jnp.where(keep, kdec, 0.0).astype(BF16), vnew_b, preferred_element_type=F32)
            s_ref[h] = s_ref[h] * cd + upd

        zc = z_ref[rows, :].astype(F32)
        z2 = jnp.concatenate([zc[:, :HEAD_DIM], zc[:, HEAD_DIM:]], axis=0)
        y = (o2 * lax.rsqrt(jnp.mean(o2 * o2, axis=-1, keepdims=True) + EPS)) * hg_ref[...]
        y = y * jax.nn.silu(z2)
        o_ref[rows, 0:HEAD_DIM] = y[:c].astype(o_ref.dtype)
        o_ref[rows, HEAD_DIM:2 * HEAD_DIM] = y[c:].astype(o_ref.dtype)
        return tq, tk, tv

    zeros = lambda n: jnp.zeros((HALO, n), F32)
    lax.fori_loop(0, seq // c, chunk, (zeros(HEAD_DIM), zeros(HEAD_DIM), zeros(2 * HEAD_DIM)))


def _delta(proj, nar, conv_w, a_log, dt_bias, head_gain, batch, seq):
    m = proj.shape[0]
    hd = HEAD_DIM
    pad = lambda v: jnp.zeros((1, LANES), F32).at[0, N_V_HEADS:2 * N_V_HEADS].set(v.astype(F32))
    qb, kb = COL_Q // hd, COL_K // hd
    vb, zb = COL_V // (2 * hd), COL_Z // (2 * hd)
    return pl.pallas_call(
        _delta_kernel,
        grid=(batch, N_QK_HEADS),
        in_specs=[
            pl.BlockSpec((seq, hd), lambda b, h: (b, qb + h)),
            pl.BlockSpec((seq, hd), lambda b, h: (b, kb + h)),
            pl.BlockSpec((seq, 2 * hd), lambda b, h: (b, vb + h)),
            pl.BlockSpec((seq, 2 * hd), lambda b, h: (b, zb + h)),
            pl.BlockSpec((seq, LANES), lambda b, h: (b, 0)),
            pl.BlockSpec((CONV_A, hd), lambda b, h: (0, qb + h)),
            pl.BlockSpec((CONV_A, hd), lambda b, h: (0, kb + h)),
            pl.BlockSpec((CONV_A, 2 * hd), lambda b, h: (0, vb + h)),
            pl.BlockSpec((1, LANES), lambda b, h: (0, 0)),
            pl.BlockSpec((1, LANES), lambda b, h: (0, 0)),
            pl.BlockSpec((1, hd), lambda b, h: (0, 0)),
        ],
        out_specs=pl.BlockSpec((seq, 2 * hd), lambda b, h: (b, h)),
        out_shape=jax.ShapeDtypeStruct((m, V_WIDTH), BF16),
        scratch_shapes=[pltpu.VMEM((2, hd, hd), F32)],
        compiler_params=_params(("parallel", "parallel")),
        name="delta",
    )(proj, proj, proj, proj, nar, conv_w, conv_w, conv_w, pad(a_log), pad(dt_bias), head_gain.reshape(1, hd))


def _sgu_kernel(u_ref, v_ref, gain_ref, w_ref, b_ref, o_ref):
    t = CHUNK_B
    g_dim = GROUP_DIM_B
    row = lax.broadcasted_iota(jnp.int32, (t, t), 0)
    col = lax.broadcasted_iota(jnp.int32, (t, t), 1)
    keep = row >= col

    def chunk(ci, carry):
        rows = pl.ds(pl.multiple_of(ci * t, t), t)
        v = jax.nn.gelu(v_ref[rows, :].astype(F32))
        ms = jnp.mean(v * v, axis=-1, keepdims=True)
        vn = ((v * lax.rsqrt(ms + EPS)) * gain_ref[...]).astype(BF16)
        for g in range(N_GROUPS_B):
            cols = slice(g * g_dim, (g + 1) * g_dim)
            wm = jnp.where(keep, w_ref[g], 0.0).astype(BF16)
            mixed = jnp.dot(wm, vn[:, cols], preferred_element_type=F32) + b_ref[g]
            u = jax.nn.gelu(u_ref[rows, cols].astype(F32))
            o_ref[rows, cols] = (u * mixed).astype(o_ref.dtype)
        return carry

    lax.fori_loop(0, u_ref.shape[0] // t, chunk, 0)


def _sgu(proj, gain, w_s, b_s, tr):
    m = proj.shape[0]
    wb = WIDTH_B
    b_bc = jnp.broadcast_to(b_s.astype(F32)[:, :, None], (N_GROUPS_B, CHUNK_B, GROUP_DIM_B))
    return pl.pallas_call(
        _sgu_kernel,
        grid=(m // tr,),
        in_specs=[
            pl.BlockSpec((tr, wb), lambda i: (i, COL_U // wb)),
            pl.BlockSpec((tr, wb), lambda i: (i, COL_VB // wb)),
            pl.BlockSpec((1, wb), lambda i: (0, 0)),
            pl.BlockSpec((N_GROUPS_B, CHUNK_B, CHUNK_B), lambda i: (0, 0, 0)),
            pl.BlockSpec((N_GROUPS_B, CHUNK_B, GROUP_DIM_B), lambda i: (0, 0, 0)),
        ],
        out_specs=pl.BlockSpec((tr, wb), lambda i: (i, 0)),
        out_shape=jax.ShapeDtypeStruct((m, wb), BF16),
        compiler_params=_params(("parallel",)),
        name="sgu",
    )(proj, proj, gain.reshape(1, wb), w_s, b_bc)


def _merge_kernel(o_ref, s_ref, ga_ref, gb_ref, wa_ref, wb_ref, out_ref):
    ya = jnp.dot(o_ref[...], wa_ref[...], preferred_element_type=F32)
    yb = jnp.dot(s_ref[...], wb_ref[...], preferred_element_type=F32)
    ga = jax.nn.sigmoid(ga_ref[...].astype(F32))
    gb = jax.nn.sigmoid(gb_ref[...].astype(F32))
    out_ref[...] = (ga * ya + gb * yb).astype(out_ref.dtype)


def _merge(o, s, proj, wa, wb, tm, tn):
    m = o.shape[0]
    d = wa.shape[1]
    ga0 = COL_GA // tn
    gb0 = (COL_GA + d) // tn
    return pl.pallas_call(
        _merge_kernel,
        grid=(m // tm, d // tn),
        in_specs=[
            pl.BlockSpec((tm, o.shape[1]), lambda i, j: (i, 0)),
            pl.BlockSpec((tm, s.shape[1]), lambda i, j: (i, 0)),
            pl.BlockSpec((tm, tn), lambda i, j: (i, ga0 + j)),
            pl.BlockSpec((tm, tn), lambda i, j: (i, gb0 + j)),
            pl.BlockSpec((wa.shape[0], tn), lambda i, j: (0, j)),
            pl.BlockSpec((wb.shape[0], tn), lambda i, j: (0, j)),
        ],
        out_specs=pl.BlockSpec((tm, tn), lambda i, j: (i, j)),
        out_shape=jax.ShapeDtypeStruct((m, d), BF16),
        compiler_params=_params(("parallel", "arbitrary")),
        name="merge",
    )(o, s, proj, proj, wa, wb)


def _mm_res_kernel(a_ref, w_ref, res_ref, out_ref):
    out_ref[...] = res_ref[...] + jnp.dot(a_ref[...], w_ref[...], preferred_element_type=F32)


def _mm_res(a, w, res, tm, tn, name):
    m, k = a.shape
    n = w.shape[1]
    return pl.pallas_call(
        _mm_res_kernel,
        grid=(m // tm, n // tn),
        in_specs=[
            pl.BlockSpec((tm, k), lambda i, j: (i, 0)),
            pl.BlockSpec((k, tn), lambda i, j: (0, j)),
            pl.BlockSpec((tm, tn), lambda i, j: (i, j)),
        ],
        out_specs=pl.BlockSpec((tm, tn), lambda i, j: (i, j)),
        out_shape=jax.ShapeDtypeStruct((m, n), F32),
        compiler_params=_params(("parallel", "arbitrary")),
        name=name,
    )(a, w, res)


def _ffn_up_kernel(blocks_per_seq, x_ref, gain_ref, wg_ref, wv_ref, cg_ref, cv_ref, bg_ref, bv_ref, out_ref,
                   h_ref, ug_ref, uv_ref, tg_ref, tv_ref):
    i = pl.program_id(0)
    j = pl.program_id(1)
    tm = out_ref.shape[0]

    @pl.when(j == 0)
    def _():
        _rms_rows_to(x_ref, gain_ref, h_ref)

    first = (i % blocks_per_seq) == 0
    h = h_ref[...]
    for w_ref, u_ref, t_ref in ((wg_ref, ug_ref, tg_ref), (wv_ref, uv_ref, tv_ref)):
        u_ref[0:HALO, :] = jnp.where(first, 0.0, t_ref[j])
        u_ref[HALO:, :] = jnp.dot(h, w_ref[...], preferred_element_type=F32)
        t_ref[j] = u_ref[tm:tm + HALO, :]

    sub = 64
    for r in range(tm // sub):
        def conv(u_ref, c_ref, b_ref):
            acc = b_ref[...] + c_ref[CONV_FFN - 1:CONV_FFN, :] * u_ref[HALO + r * sub:HALO + (r + 1) * sub, :]
            for t in range(CONV_FFN - 1):
                off = HALO - (CONV_FFN - 1) + t + r * sub
                acc = acc + c_ref[t:t + 1, :] * u_ref[off:off + sub, :]
            return acc
        gate = conv(ug_ref, cg_ref, bg_ref)
        val = conv(uv_ref, cv_ref, bv_ref)
        out_ref[r * sub:(r + 1) * sub, :] = (jax.nn.silu(gate) * val).astype(out_ref.dtype)


def _ffn_up(xf, gain, w_up, conv_w, conv_b, seq, tm, tn):
    m, d = xf.shape
    nj = D_FF // tn
    b2 = conv_b.reshape(1, 2 * D_FF).astype(F32)
    return pl.pallas_call(
        functools.partial(_ffn_up_kernel, seq // tm),
        grid=(m // tm, nj),
        in_specs=[
            pl.BlockSpec((tm, d), lambda i, j: (i, 0)),
            pl.BlockSpec((1, d), lambda i, j: (0, 0)),
            pl.BlockSpec((d, tn), lambda i, j: (0, j)),
            pl.BlockSpec((d, tn), lambda i, j: (0, nj + j)),
            pl.BlockSpec((CONV_FFN, tn), lambda i, j: (0, j)),
            pl.BlockSpec((CONV_FFN, tn), lambda i, j: (0, nj + j)),
            pl.BlockSpec((1, tn), lambda i, j: (0, j)),
            pl.BlockSpec((1, tn), lambda i, j: (0, nj + j)),
        ],
        out_specs=pl.BlockSpec((tm, tn), lambda i, j: (i, j)),
        out_shape=jax.ShapeDtypeStruct((m, D_FF), BF16),
        scratch_shapes=[
            pltpu.VMEM((tm, d), BF16),
            pltpu.VMEM((tm + HALO, tn), F32),
            pltpu.VMEM((tm + HALO, tn), F32),
            pltpu.VMEM((nj, HALO, tn), F32),
            pltpu.VMEM((nj, HALO, tn), F32),
        ],
        compiler_params=_params(("arbitrary", "arbitrary")),
        name="ffn_up",
    )(xf, gain.reshape(1, d), w_up, w_up, conv_w, conv_w, b2, b2)


def _ple_kernel(x_ref, gain_ref, wg_ref, p_ref, wp_ref, out_ref, h_ref, pb_ref):
    j = pl.program_id(1)
    tn = out_ref.shape[1]

    @pl.when(j == 0)
    def _():
        _rms_rows_to(x_ref, gain_ref, h_ref)
        pb_ref[...] = p_ref[...].astype(pb_ref.dtype)

    gate = jax.nn.sigmoid(jnp.dot(h_ref[...], wg_ref[...], preferred_element_type=F32))
    emb = jnp.dot(pb_ref[...], wp_ref[...], preferred_element_type=F32)
    xres = x_ref[:, pl.ds(pl.multiple_of(j * tn, tn), tn)]
    out_ref[...] = xres + gate * emb


def _ple(xf, gain, wg, p3, layer, wp, tm, tn):
    m, d = xf.shape
    pd = p3.shape[2]
    return pl.pallas_call(
        _ple_kernel,
        grid=(m // tm, d // tn),
        in_specs=[
            pl.BlockSpec((tm, d), lambda i, j: (i, 0)),
            pl.BlockSpec((1, d), lambda i, j: (0, 0)),
            pl.BlockSpec((d, tn), lambda i, j: (0, j)),
            pl.BlockSpec((None, tm, pd), lambda i, j: (layer, i, 0)),
            pl.BlockSpec((pd, tn), lambda i, j: (0, j)),
        ],
        out_specs=pl.BlockSpec((tm, tn), lambda i, j: (i, j)),
        out_shape=jax.ShapeDtypeStruct((m, d), F32),
        scratch_shapes=[pltpu.VMEM((tm, d), BF16), pltpu.VMEM((tm, pd), BF16)],
        compiler_params=_params(("parallel", "arbitrary")),
        name="ple",
    )(xf, gain.reshape(1, d), wg, p3, wp)


def _final_norm_kernel(x_ref, gain_ref, out_ref):
    rows = 32

    def body(r, carry):
        sl = pl.ds(pl.multiple_of(r * rows, rows), rows)
        x = x_ref[sl, :]
        ms = jnp.mean(x * x, axis=-1, keepdims=True)
        out_ref[sl, :] = (x * lax.rsqrt(ms + EPS)) * gain_ref[...]
        return carry

    lax.fori_loop(0, x_ref.shape[0] // rows, body, 0)


def _final_norm(xf, gain, tm):
    m, d = xf.shape
    return pl.pallas_call(
        _final_norm_kernel,
        grid=(m // tm,),
        in_specs=[pl.BlockSpec((tm, d), lambda i: (i, 0)), pl.BlockSpec((1, d), lambda i: (0, 0))],
        out_specs=pl.BlockSpec((tm, d), lambda i: (i, 0)),
        out_shape=jax.ShapeDtypeStruct((m, d), F32),
        compiler_params=_params(("parallel",)),
        name="final_norm",
    )(xf, gain.reshape(1, d))


def kernel(x, p, norm_mix, w_in, conv_qkv, a_log, dt_bias, head_norm, sgu_norm, w_spatial, b_spatial, w_branch_a, w_branch_b, w_out, norm_ffn, w_ffn_up, conv_ffn, b_conv_ffn, w_ffn_down, norm_ple, w_ple_gate, w_ple_proj, norm_final):
    batch, seq, d = x.shape
    depth = w_in.shape[0]
    m = batch * seq
    assert d == V_WIDTH and seq % CHUNK_B == 0 and w_in.shape[2] == N_MAIN + N_NARROW
    tm = _row_tile(seq, 512)
    tn = 512
    xf = x.reshape(m, d).astype(F32)
    p3 = p.reshape(depth, m, p.shape[-1])
    for i in range(depth):
        wi = w_in[i]
        w_main = jnp.concatenate([wi[:, :COL_NARROW], wi[:, COL_NARROW + N_NARROW:]], axis=1).astype(BF16)
        w_nar = jnp.pad(wi[:, COL_NARROW:COL_NARROW + N_NARROW], ((0, 0), (0, LANES - N_NARROW))).astype(BF16)
        proj, nar = _inproj(xf, norm_mix[i], w_main, w_nar, tm, _col_tile(N_MAIN, 1024))
        o = _delta(proj, nar, conv_qkv[i].astype(F32), a_log[i], dt_bias[i], head_norm[i].astype(F32), batch, seq)
        s = _sgu(proj, sgu_norm[i].astype(F32), w_spatial[i].astype(F32), b_spatial[i], _row_tile(seq, 512))
        merged = _merge(o, s, proj, w_branch_a[i].astype(BF16), w_branch_b[i].astype(BF16), tm, tn)
        xf = _mm_res(merged, w_out[i].astype(BF16), xf, tm, tn, "out_proj")
        act = _ffn_up(xf, norm_ffn[i].astype(F32), w_ffn_up[i].astype(BF16), conv_ffn[i].astype(F32),
                      b_conv_ffn[i], seq, tm, _col_tile(D_FF, 512))
        xf = _mm_res(act, w_ffn_down[i].astype(BF16), xf, tm, tn, "ffn_down")
        xf = _ple(xf, norm_ple[i].astype(F32), w_ple_gate[i].astype(BF16), p3, i, w_ple_proj[i].astype(BF16), tm, tn)
    out = _final_norm(xf, norm_final.astype(F32), tm)
    return out.reshape(batch, seq, d).astype(x.dtype)
```

```python
import functools

import jax
import jax.numpy as jnp
from jax import lax
from jax.experimental import pallas as pl
from jax.experimental.pallas import tpu as pltpu

F32 = jnp.float32
BF16 = jnp.bfloat16
EPS = 1e-6

HEAD_DIM = 128
N_QK_HEADS = 8
N_V_HEADS = 16
QK_WIDTH = N_QK_HEADS * HEAD_DIM
V_WIDTH = N_V_HEADS * HEAD_DIM
CONV_A = 4
CHUNK_A = 64
N_GROUPS_B = 8
GROUP_DIM_B = 128
WIDTH_B = N_GROUPS_B * GROUP_DIM_B
CHUNK_B = 128
D_FF = 5632
CONV_FFN = 3

COL_Q = 0
COL_K = QK_WIDTH
COL_V = 2 * QK_WIDTH
COL_Z = COL_V + V_WIDTH
COL_NARROW = COL_Z + V_WIDTH
N_NARROW = 2 * N_V_HEADS
COL_U = COL_Z + V_WIDTH
COL_VB = COL_U + WIDTH_B
COL_GA = COL_VB + WIDTH_B
N_MAIN = COL_GA + 2 * 2048
LANES = 128
HALO = 8

V7X_VMEM_LIMIT_BYTES = 52 * 1024 * 1024


def _params(sem):
    return pltpu.CompilerParams(dimension_semantics=sem, vmem_limit_bytes=V7X_VMEM_LIMIT_BYTES)


def _row_tile(s, target):
    t = min(target, s)
    while s % t:
        t //= 2
    return t


def _col_tile(n, target):
    t = min(target, n)
    while n % t or t % LANES:
        t -= LANES
    return t


def _rms_rows_to(x_ref, gain_ref, h_ref, rows=32):
    tm = x_ref.shape[0]

    def body(r, carry):
        sl = pl.ds(pl.multiple_of(r * rows, rows), rows)
        x = x_ref[sl, :]
        ms = jnp.mean(x * x, axis=-1, keepdims=True)
        h_ref[sl, :] = ((x * lax.rsqrt(ms + EPS)) * gain_ref[...]).astype(h_ref.dtype)
        return carry

    lax.fori_loop(0, tm // rows, body, 0)


def _inproj_kernel(x_ref, gain_ref, w_ref, wn_ref, out_ref, nar_ref, h_ref):
    @pl.when(pl.program_id(1) == 0)
    def _():
        _rms_rows_to(x_ref, gain_ref, h_ref)
        nar_ref[...] = jnp.dot(h_ref[...], wn_ref[...], preferred_element_type=F32)

    out_ref[...] = jnp.dot(h_ref[...], w_ref[...], preferred_element_type=F32).astype(out_ref.dtype)


def _inproj(xf, gain, w_main, w_nar, tm, tn):
    m, d = xf.shape
    n = w_main.shape[1]
    return pl.pallas_call(
        _inproj_kernel,
        grid=(m // tm, n // tn),
        in_specs=[
            pl.BlockSpec((tm, d), lambda i, j: (i, 0)),
            pl.BlockSpec((1, d), lambda i, j: (0, 0)),
            pl.BlockSpec((d, tn), lambda i, j: (0, j)),
            pl.BlockSpec((d, LANES), lambda i, j: (0, 0)),
        ],
        out_specs=[
            pl.BlockSpec((tm, tn), lambda i, j: (i, j)),
            pl.BlockSpec((tm, LANES), lambda i, j: (i, 0)),
        ],
        out_shape=[jax.ShapeDtypeStruct((m, n), BF16), jax.ShapeDtypeStruct((m, LANES), F32)],
        scratch_shapes=[pltpu.VMEM((tm, d), BF16)],
        compiler_params=_params(("parallel", "arbitrary")),
        name="inproj",
    )(xf, gain.reshape(1, d), w_main, w_nar)


def _softplus(x):
    return jnp.maximum(x, 0.0) + jnp.log(1.0 + jnp.exp(-jnp.abs(x)))


def _delta_kernel(nh, group, q_ref, k_ref, v_ref, z_ref, nar_ref, cq_ref, ck_ref, cv_ref, alog_ref, dtb_ref, hg_ref,
                  o_ref, s_ref, cs_ref, lhs_ref, kt_ref, bd_ref, dc_ref, x0_ref, wq_ref, u_ref, at_ref, kd_ref, cd_ref):
    c = CHUNK_A
    hd = HEAD_DIM
    n_groups = q_ref.shape[0] // (group * c)
    hq0 = pl.program_id(1) * nh
    for ref in (s_ref, cs_ref, lhs_ref, kt_ref, bd_ref, dc_ref, x0_ref, wq_ref, u_ref, at_ref, kd_ref, cd_ref):
        ref[...] = jnp.zeros_like(ref)

    row = lax.broadcasted_iota(jnp.int32, (2 * c, 2 * c), 0)
    col = lax.broadcasted_iota(jnp.int32, (2 * c, 2 * c), 1)
    same_head = (row < c) == (col < c)
    causal = same_head & (row >= col)
    strict = same_head & (row > col)
    lane_lo = col < c
    lane_m = col - jnp.where(row < c, 0, 1)
    row_m = row - jnp.where(lane_lo, 0, 1)
    row_c = lax.broadcasted_iota(jnp.int32, (c, LANES), 0)
    neg_exp_alog = -jnp.exp(alog_ref[...])
    q_scale = HEAD_DIM ** -0.5
    heads = [(hh, h) for hh in range(nh) for h in range(2)]

    def dots(lhs_list, rhs_list):
        return [jnp.dot(a, b, preferred_element_type=F32) for a, b in zip(lhs_list, rhs_list)]

    def gates(ci):
        rows = pl.ds(pl.multiple_of(ci * c, c), c)
        nar = nar_ref[rows, :]
        sig = jax.nn.sigmoid(nar)
        gam = neg_exp_alog * _softplus(nar + dtb_ref[...])
        for sh in (1, 2, 4, 8, 16, 32):
            gam = gam + jnp.where(row_c >= sh, pltpu.roll(gam, sh, axis=0), 0.0)
        sig2 = jnp.concatenate([sig, sig], axis=0)
        gam2 = jnp.concatenate([gam, gam], axis=0)
        return sig2, gam2, gam2.T

    def conv_group(c0, hh):
        n = group * c
        rows = pl.ds(pl.multiple_of(c0 * c, c), n)
        srcs = ((q_ref, cq_ref, hh * hd), (k_ref, ck_ref, hh * hd),
                (v_ref, cv_ref, hh * 2 * hd), (v_ref, cv_ref, hh * 2 * hd + hd))
        outs = []
        for s, (x_ref, w_ref, c_lo) in enumerate(srcs):
            cols = slice(c_lo, c_lo + hd)
            cs_ref[hh, s, HALO:HALO + n, :] = x_ref[rows, cols].astype(F32)
            per_chunk = []
            for g in range(group):
                acc = w_ref[CONV_A - 1:CONV_A, cols] * cs_ref[hh, s, HALO + g * c:HALO + (g + 1) * c, :]
                for j in range(CONV_A - 1):
                    off = HALO - (CONV_A - 1) + j + g * c
                    acc = acc + w_ref[j:j + 1, cols] * cs_ref[hh, s, pl.ds(off, c, stride=1), :]
                per_chunk.append(jax.nn.silu(acc))
            outs.append(per_chunk)
            cs_ref[hh, s, 0:HALO, :] = cs_ref[hh, s, n:n + HALO, :]
        return outs

    def prep(grp, par, ring):
        c0 = grp * group
        conv = [conv_group(c0, hh) for hh in range(nh)]
        for g in range(group):
            sig2, gam2, gam2t = gates(c0 + g)
            for hh in range(nh):
                hq2 = 2 * (hq0 + hh)
                qc, kc, v_lo, v_hi = (conv[hh][s][g] for s in range(4))
                qn = qc * (lax.rsqrt(jnp.sum(qc * qc, axis=-1, keepdims=True) + EPS) * q_scale)
                kn = kc * lax.rsqrt(jnp.sum(kc * kc, axis=-1, keepdims=True) + EPS)
                beta_c = jnp.sum(jnp.where(lane_m == hq2, sig2, 0.0), axis=1, keepdims=True)
                gam_c = jnp.sum(jnp.where(lane_m == hq2 + N_V_HEADS, gam2, 0.0), axis=1, keepdims=True)
                gam_r = jnp.sum(jnp.where(row_m == hq2 + N_V_HEADS, gam2t, 0.0), axis=0, keepdims=True)
                gam_cb = jnp.broadcast_to(gam_c, (2 * c, 2 * c))
                decay = jnp.where(causal, jnp.exp(jnp.where(causal, gam_cb - gam_r, 0.0)), 0.0)
                egam_c = jnp.exp(gam_c)
                glast_r = jnp.where(lane_lo[0:1, :], gam_cb[c - 1:c, :], gam_cb[2 * c - 1:2 * c, :])
                kn2 = jnp.concatenate([kn, kn], axis=0)
                kt2 = kn2.T
                kdec = kt2 * jnp.exp(glast_r - gam_r)
                qd2 = jnp.concatenate([qn, qn], axis=0) * egam_c
                slot = ring * group + g
                for h in range(2):
                    keep = lane_lo if h == 0 else jnp.logical_not(lane_lo)
                    kd_ref[slot, hh, h] = jnp.where(keep, kdec, 0.0).astype(BF16)
                    cd = jnp.exp(gam_cb[(h + 1) * c - 1:(h + 1) * c, :])
                    cd_ref[slot, hh, h] = jnp.broadcast_to(cd, (HALO, LANES))
                    wq_ref[slot, hh, h, c:2 * c, :] = qd2[h * c:(h + 1) * c].astype(BF16)
                v2 = jnp.concatenate([v_lo, v_hi], axis=0)
                lhs_ref[par, g, hh] = jnp.concatenate([qn, kn], axis=0).astype(BF16)
                kt_ref[par, g, hh] = kt2.astype(BF16)
                dc_ref[par, g, hh] = decay
                bd_ref[par, g, hh] = jnp.where(strict, beta_c * decay, 0.0)
                x0_ref[par, g, hh] = jnp.concatenate([v2 * beta_c, kn2 * (beta_c * egam_c)], axis=1)

    def solve(par, ring):
        items = [(g, hh) for g in range(group) for hh in range(nh)]
        rs = dots([lhs_ref[par, g, hh] for g, hh in items], [kt_ref[par, g, hh] for g, hh in items])
        yield
        pws, xs = [], []
        for (g, hh), r in zip(items, rs):
            qk2 = jnp.concatenate([r[:c], r[:c]], axis=0)
            kk2 = jnp.concatenate([r[c:], r[c:]], axis=0)
            at_ref[ring * group + g, hh] = (qk2 * dc_ref[par, g, hh]).astype(BF16)
            pws.append((-(kk2 * bd_ref[par, g, hh])).astype(BF16))
            xs.append(x0_ref[par, g, hh])
        for level in range(6):
            if level:
                pws = [p.astype(BF16) for p in dots(pws, pws)]
            xs = [x + d for x, d in zip(xs, dots(pws, [x.astype(BF16) for x in xs]))]
            yield
        for (g, hh), x in zip(items, xs):
            slot = ring * group + g
            u_ref[slot, hh] = x[:, :hd]
            for h in range(2):
                wq_ref[slot, hh, h, 0:c, :] = x[h * c:(h + 1) * c, hd:].astype(BF16)

    def scan(grp, ring):
        for g in range(group):
            slot = ring * group + g
            rows = pl.ds(pl.multiple_of((grp * group + g) * c, c), c)
            res = dots([wq_ref[slot, hh, h] for hh, h in heads], [s_ref[hh, h].astype(BF16) for hh, h in heads])
            yield
            vnew_b, o_state = [], []
            for hh in range(nh):
                u = u_ref[slot, hh]
                r0, r1 = res[2 * hh], res[2 * hh + 1]
                vnew_b.append(jnp.concatenate([u[:c] - r0[:c], u[c:] - r1[:c]], axis=0).astype(BF16))
                o_state.append(jnp.concatenate([r0[c:], r1[c:]], axis=0))
            upd = dots([kd_ref[slot, hh, h] for hh, h in heads], [vnew_b[hh] for hh, h in heads])
            o_intra = dots([at_ref[slot, hh] for hh in range(nh)], vnew_b)
            yield
            for (hh, h), d in zip(heads, upd):
                s_ref[hh, h] = s_ref[hh, h] * cd_ref[slot, hh, h][0:1, :] + d
            for hh in range(nh):
                o2 = o_state[hh] + o_intra[hh]
                zc = z_ref[rows, hh * 2 * hd:(hh + 1) * 2 * hd].astype(F32)
                z2 = jnp.concatenate([zc[:, :hd], zc[:, hd:]], axis=0)
                y = (o2 * lax.rsqrt(jnp.mean(o2 * o2, axis=-1, keepdims=True) + EPS)) * hg_ref[...]
                y = y * jax.nn.silu(z2)
                o_ref[rows, hh * 2 * hd:hh * 2 * hd + hd] = y[:c].astype(o_ref.dtype)
                o_ref[rows, hh * 2 * hd + hd:(hh + 1) * 2 * hd] = y[c:].astype(o_ref.dtype)

    def body(t, carry):
        last = n_groups - 1
        scan_it = scan(jnp.clip(t - 2, 0, last), lax.rem(t + 1, 3))
        solve_it = solve(lax.rem(t + 1, 2), lax.rem(t + 2, 3))
        for turn in range(2 * group + 7):
            if turn % 2 == 0 or turn >= 4 * group:
                next(solve_it, None)
            else:
                next(scan_it, None)
        for _ in scan_it:
            pass
        for _ in solve_it:
            pass
        prep(jnp.minimum(t, last), lax.rem(t, 2), lax.rem(t, 3))
        return carry

    lax.fori_loop(0, n_groups + 2, body, 0)


def _delta(proj, nar, conv_w, a_log, dt_bias, head_gain, batch, seq, nh=2, group=2):
    m = proj.shape[0]
    hd = HEAD_DIM
    c2 = 2 * CHUNK_A
    assert N_QK_HEADS % nh == 0 and seq % (group * CHUNK_A) == 0
    pad = lambda v: jnp.zeros((1, LANES), F32).at[0, N_V_HEADS:2 * N_V_HEADS].set(v.astype(F32))
    qw, vw = nh * hd, nh * 2 * hd
    qb, kb, vb, zb = COL_Q // qw, COL_K // qw, COL_V // vw, COL_Z // vw
    ring = 3 * group
    return pl.pallas_call(
        functools.partial(_delta_kernel, nh, group),
        grid=(batch, N_QK_HEADS // nh),
        in_specs=[
            pl.BlockSpec((seq, qw), lambda b, h: (b, qb + h)),
            pl.BlockSpec((seq, qw), lambda b, h: (b, kb + h)),
            pl.BlockSpec((seq, vw), lambda b, h: (b, vb + h)),
            pl.BlockSpec((seq, vw), lambda b, h: (b, zb + h)),
            pl.BlockSpec((seq, LANES), lambda b, h: (b, 0)),
            pl.BlockSpec((CONV_A, qw), lambda b, h: (0, qb + h)),
            pl.BlockSpec((CONV_A, qw), lambda b, h: (0, kb + h)),
            pl.BlockSpec((CONV_A, vw), lambda b, h: (0, vb + h)),
            pl.BlockSpec((1, LANES), lambda b, h: (0, 0)),
            pl.BlockSpec((1, LANES), lambda b, h: (0, 0)),
            pl.BlockSpec((1, hd), lambda b, h: (0, 0)),
        ],
        out_specs=pl.BlockSpec((seq, vw), lambda b, h: (b, h)),
        out_shape=jax.ShapeDtypeStruct((m, V_WIDTH), BF16),
        scratch_shapes=[
            pltpu.VMEM((nh, 2, hd, hd), F32),
            pltpu.VMEM((nh, 4, 2 * HALO + group * CHUNK_A, LANES), F32),
            pltpu.VMEM((2, group, nh, c2, hd), BF16),
            pltpu.VMEM((2, group, nh, hd, c2), BF16),
            pltpu.VMEM((2, group, nh, c2, c2), F32),
            pltpu.VMEM((2, group, nh, c2, c2), F32),
            pltpu.VMEM((2, group, nh, c2, 2 * hd), F32),
            pltpu.VMEM((ring, nh, 2, c2, hd), BF16),
            pltpu.VMEM((ring, nh, c2, hd), F32),
            pltpu.VMEM((ring, nh, c2, c2), BF16),
            pltpu.VMEM((ring, nh, 2, hd, c2), BF16),
            pltpu.VMEM((ring, nh, 2, HALO, LANES), F32),
        ],
        compiler_params=_params(("parallel", "parallel")),
        name="delta",
    )(proj, proj, proj, proj, nar, conv_w, conv_w, conv_w, pad(a_log), pad(dt_bias), head_gain.reshape(1, hd))


def _sgu_kernel(u_ref, v_ref, gain_ref, w_ref, b_ref, o_ref):
    t = CHUNK_B
    g_dim = GROUP_DIM_B
    row = lax.broadcasted_iota(jnp.int32, (t, t), 0)
    col = lax.broadcasted_iota(jnp.int32, (t, t), 1)
    keep = row >= col

    def chunk(ci, carry):
        rows = pl.ds(pl.multiple_of(ci * t, t), t)
        v = jax.nn.gelu(v_ref[rows, :].astype(F32))
        ms = jnp.mean(v * v, axis=-1, keepdims=True)
        vn = ((v * lax.rsqrt(ms + EPS)) * gain_ref[...]).astype(BF16)
        for g in range(N_GROUPS_B):
            cols = slice(g * g_dim, (g + 1) * g_dim)
            wm = jnp.where(keep, w_ref[g], 0.0).astype(BF16)
            mixed = jnp.dot(wm, vn[:, cols], preferred_element_type=F32) + b_ref[g]
            u = jax.nn.gelu(u_ref[rows, cols].astype(F32))
            o_ref[rows, cols] = (u * mixed).astype(o_ref.dtype)
        return carry

    lax.fori_loop(0, u_ref.shape[0] // t, chunk, 0)


def _sgu(proj, gain, w_s, b_s, tr):
    m = proj.shape[0]
    wb = WIDTH_B
    b_bc = jnp.broadcast_to(b_s.astype(F32)[:, :, None], (N_GROUPS_B, CHUNK_B, GROUP_DIM_B))
    return pl.pallas_call(
        _sgu_kernel,
        grid=(m // tr,),
        in_specs=[
            pl.BlockSpec((tr, wb), lambda i: (i, COL_U // wb)),
            pl.BlockSpec((tr, wb), lambda i: (i, COL_VB // wb)),
            pl.BlockSpec((1, wb), lambda i: (0, 0)),
            pl.BlockSpec((N_GROUPS_B, CHUNK_B, CHUNK_B), lambda i: (0, 0, 0)),
            pl.BlockSpec((N_GROUPS_B, CHUNK_B, GROUP_DIM_B), lambda i: (0, 0, 0)),
        ],
        out_specs=pl.BlockSpec((tr, wb), lambda i: (i, 0)),
        out_shape=jax.ShapeDtypeStruct((m, wb), BF16),
        compiler_params=_params(("parallel",)),
        name="sgu",
    )(proj, proj, gain.reshape(1, wb), w_s, b_bc)


def _merge_kernel(o_ref, s_ref, ga_ref, gb_ref, wa_ref, wb_ref, out_ref):
    ya = jnp.dot(o_ref[...], wa_ref[...], preferred_element_type=F32)
    yb = jnp.dot(s_ref[...], wb_ref[...], preferred_element_type=F32)
    ga = jax.nn.sigmoid(ga_ref[...].astype(F32))
    gb = jax.nn.sigmoid(gb_ref[...].astype(F32))
    out_ref[...] = (ga * ya + gb * yb).astype(out_ref.dtype)


def _merge(o, s, proj, wa, wb, tm, tn):
    m = o.shape[0]
    d = wa.shape[1]
    ga0 = COL_GA // tn
    gb0 = (COL_GA + d) // tn
    return pl.pallas_call(
        _merge_kernel,
        grid=(m // tm, d // tn),
        in_specs=[
            pl.BlockSpec((tm, o.shape[1]), lambda i, j: (i, 0)),
            pl.BlockSpec((tm, s.shape[1]), lambda i, j: (i, 0)),
            pl.BlockSpec((tm, tn), lambda i, j: (i, ga0 + j)),
            pl.BlockSpec((tm, tn), lambda i, j: (i, gb0 + j)),
            pl.BlockSpec((wa.shape[0], tn), lambda i, j: (0, j)),
            pl.BlockSpec((wb.shape[0], tn), lambda i, j: (0, j)),
        ],
        out_specs=pl.BlockSpec((tm, tn), lambda i, j: (i, j)),
        out_shape=jax.ShapeDtypeStruct((m, d), BF16),
        compiler_params=_params(("parallel", "arbitrary")),
        name="merge",
    )(o, s, proj, proj, wa, wb)


def _mm_res_kernel(a_ref, w_ref, res_ref, out_ref):
    out_ref[...] = res_ref[...] + jnp.dot(a_ref[...], w_ref[...], preferred_element_type=F32)


def _mm_res(a, w, res, tm, tn, name):
    m, k = a.shape
    n = w.shape[1]
    return pl.pallas_call(
        _mm_res_kernel,
        grid=(m // tm, n // tn),
        in_specs=[
            pl.BlockSpec((tm, k), lambda i, j: (i, 0)),
            pl.BlockSpec((k, tn), lambda i, j: (0, j)),
            pl.BlockSpec((tm, tn), lambda i, j: (i, j)),
        ],
        out_specs=pl.BlockSpec((tm, tn), lambda i, j: (i, j)),
        out_shape=jax.ShapeDtypeStruct((m, n), F32),
        compiler_params=_params(("parallel", "arbitrary")),
        name=name,
    )(a, w, res)


def _ffn_up_kernel(blocks_per_seq, x_ref, gain_ref, wg_ref, wv_ref, cg_ref, cv_ref, bg_ref, bv_ref, out_ref,
                   h_ref, ug_ref, uv_ref, tg_ref, tv_ref):
    i = pl.program_id(0)
    j = pl.program_id(1)
    tm = out_ref.shape[0]

    @pl.when(j == 0)
    def _():
        _rms_rows_to(x_ref, gain_ref, h_ref)

    first = (i % blocks_per_seq) == 0
    h = h_ref[...]
    for w_ref, u_ref, t_ref in ((wg_ref, ug_ref, tg_ref), (wv_ref, uv_ref, tv_ref)):
        u_ref[0:HALO, :] = jnp.where(first, 0.0, t_ref[j])
        u_ref[HALO:, :] = jnp.dot(h, w_ref[...], preferred_element_type=F32)
        t_ref[j] = u_ref[tm:tm + HALO, :]

    sub = 64
    for r in range(tm // sub):
        def conv(u_ref, c_ref, b_ref):
            acc = b_ref[...] + c_ref[CONV_FFN - 1:CONV_FFN, :] * u_ref[HALO + r * sub:HALO + (r + 1) * sub, :]
            for t in range(CONV_FFN - 1):
                off = HALO - (CONV_FFN - 1) + t + r * sub
                acc = acc + c_ref[t:t + 1, :] * u_ref[off:off + sub, :]
            return acc
        gate = conv(ug_ref, cg_ref, bg_ref)
        val = conv(uv_ref, cv_ref, bv_ref)
        out_ref[r * sub:(r + 1) * sub, :] = (jax.nn.silu(gate) * val).astype(out_ref.dtype)


def _ffn_up(xf, gain, w_up, conv_w, conv_b, seq, tm, tn):
    m, d = xf.shape
    nj = D_FF // tn
    b2 = conv_b.reshape(1, 2 * D_FF).astype(F32)
    return pl.pallas_call(
        functools.partial(_ffn_up_kernel, seq // tm),
        grid=(m // tm, nj),
        in_specs=[
            pl.BlockSpec((tm, d), lambda i, j: (i, 0)),
            pl.BlockSpec((1, d), lambda i, j: (0, 0)),
            pl.BlockSpec((d, tn), lambda i, j: (0, j)),
            pl.BlockSpec((d, tn), lambda i, j: (0, nj + j)),
            pl.BlockSpec((CONV_FFN, tn), lambda i, j: (0, j)),
            pl.BlockSpec((CONV_FFN, tn), lambda i, j: (0, nj + j)),
            pl.BlockSpec((1, tn), lambda i, j: (0, j)),
            pl.BlockSpec((1, tn), lambda i, j: (0, nj + j)),
        ],
        out_specs=pl.BlockSpec((tm, tn), lambda i, j: (i, j)),
        out_shape=jax.ShapeDtypeStruct((m, D_FF), BF16),
        scratch_shapes=[
            pltpu.VMEM((tm, d), BF16),
            pltpu.VMEM((tm + HALO, tn), F32),
            pltpu.VMEM((tm + HALO, tn), F32),
            pltpu.VMEM((nj, HALO, tn), F32),
            pltpu.VMEM((nj, HALO, tn), F32),
        ],
        compiler_params=_params(("arbitrary", "arbitrary")),
        name="ffn_up",
    )(xf, gain.reshape(1, d), w_up, w_up, conv_w, conv_w, b2, b2)


def _ple_kernel(x_ref, gain_ref, wg_ref, p_ref, wp_ref, out_ref, h_ref, pb_ref):
    j = pl.program_id(1)
    tn = out_ref.shape[1]

    @pl.when(j == 0)
    def _():
        _rms_rows_to(x_ref, gain_ref, h_ref)
        pb_ref[...] = p_ref[...].astype(pb_ref.dtype)

    gate = jax.nn.sigmoid(jnp.dot(h_ref[...], wg_ref[...], preferred_element_type=F32))
    emb = jnp.dot(pb_ref[...], wp_ref[...], preferred_element_type=F32)
    xres = x_ref[:, pl.ds(pl.multiple_of(j * tn, tn), tn)]
    out_ref[...] = xres + gate * emb


def _ple(xf, gain, wg, p3, layer, wp, tm, tn):
    m, d = xf.shape
    pd = p3.shape[2]
    return pl.pallas_call(
        _ple_kernel,
        grid=(m // tm, d // tn),
        in_specs=[
            pl.BlockSpec((tm, d), lambda i, j: (i, 0)),
            pl.BlockSpec((1, d), lambda i, j: (0, 0)),
            pl.BlockSpec((d, tn), lambda i, j: (0, j)),
            pl.BlockSpec((None, tm, pd), lambda i, j: (layer, i, 0)),
            pl.BlockSpec((pd, tn), lambda i, j: (0, j)),
        ],
        out_specs=pl.BlockSpec((tm, tn), lambda i, j: (i, j)),
        out_shape=jax.ShapeDtypeStruct((m, d), F32),
        scratch_shapes=[pltpu.VMEM((tm, d), BF16), pltpu.VMEM((tm, pd), BF16)],
        compiler_params=_params(("parallel", "arbitrary")),
        name="ple",
    )(xf, gain.reshape(1, d), wg, p3, wp)


def _final_norm_kernel(x_ref, gain_ref, out_ref):
    rows = 32

    def body(r, carry):
        sl = pl.ds(pl.multiple_of(r * rows, rows), rows)
        x = x_ref[sl, :]
        ms = jnp.mean(x * x, axis=-1, keepdims=True)
        out_ref[sl, :] = (x * lax.rsqrt(ms + EPS)) * gain_ref[...]
        return carry

    lax.fori_loop(0, x_ref.shape[0] // rows, body, 0)


def _final_norm(xf, gain, tm):
    m, d = xf.shape
    return pl.pallas_call(
        _final_norm_kernel,
        grid=(m // tm,),
        in_specs=[pl.BlockSpec((tm, d), lambda i: (i, 0)), pl.BlockSpec((1, d), lambda i: (0, 0))],
        out_specs=pl.BlockSpec((tm, d), lambda i: (i, 0)),
        out_shape=jax.ShapeDtypeStruct((m, d), F32),
        compiler_params=_params(("parallel",)),
        name="final_norm",
    )(xf, gain.reshape(1, d))


def kernel(x, p, norm_mix, w_in, conv_qkv, a_log, dt_bias, head_norm, sgu_norm, w_spatial, b_spatial, w_branch_a, w_branch_b, w_out, norm_ffn, w_ffn_up, conv_ffn, b_conv_ffn, w_ffn_down, norm_ple, w_ple_gate, w_ple_proj, norm_final):
    batch, seq, d = x.shape
    depth = w_in.shape[0]
    m = batch * seq
    assert d == V_WIDTH and seq % CHUNK_B == 0 and w_in.shape[2] == N_MAIN + N_NARROW
    tm = _row_tile(seq, 512)
    tn = 512
    xf = x.reshape(m, d).astype(F32)
    p3 = p.reshape(depth, m, p.shape[-1])
    for i in range(depth):
        wi = w_in[i]
        w_main = jnp.concatenate([wi[:, :COL_NARROW], wi[:, COL_NARROW + N_NARROW:]], axis=1).astype(BF16)
        w_nar = jnp.pad(wi[:, COL_NARROW:COL_NARROW + N_NARROW], ((0, 0), (0, LANES - N_NARROW))).astype(BF16)
        proj, nar = _inproj(xf, norm_mix[i], w_main, w_nar, tm, _col_tile(N_MAIN, 1024))
        o = _delta(proj, nar, conv_qkv[i].astype(F32), a_log[i], dt_bias[i], head_norm[i].astype(F32), batch, seq)
        s = _sgu(proj, sgu_norm[i].astype(F32), w_spatial[i].astype(F32), b_spatial[i], _row_tile(seq, 512))
        merged = _merge(o, s, proj, w_branch_a[i].astype(BF16), w_branch_b[i].astype(BF16), tm, tn)
        xf = _mm_res(merged, w_out[i].astype(BF16), xf, tm, tn, "out_proj")
        act = _ffn_up(xf, norm_ffn[i].astype(F32), w_ffn_up[i].astype(BF16), conv_ffn[i].astype(F32),
                      b_conv_ffn[i], seq, tm, _col_tile(D_FF, 512))
        xf = _mm_res(act, w_ffn_down[i].astype(BF16), xf, tm, tn, "ffn_down")
        xf = _ple(xf, norm_ple[i].astype(F32), w_ple_gate[i].astype(BF16), p3, i, w_ple_proj[i].astype(BF16), tm, tn)
    out = _final_norm(xf, norm_final.astype(F32), tm)
    return out.reshape(batch, seq, d).astype(x.dtype)
```

```python
import functools

import jax
import jax.numpy as jnp
from jax import lax
from jax.experimental import pallas as pl
from jax.experimental.pallas import tpu as pltpu

F32 = jnp.float32
BF16 = jnp.bfloat16
EPS = 1e-6

HEAD_DIM = 128
N_QK_HEADS = 8
N_V_HEADS = 16
QK_WIDTH = N_QK_HEADS * HEAD_DIM
V_WIDTH = N_V_HEADS * HEAD_DIM
CONV_A = 4
CHUNK_A = 64
N_GROUPS_B = 8
GROUP_DIM_B = 128
WIDTH_B = N_GROUPS_B * GROUP_DIM_B
CHUNK_B = 128
D_FF = 5632
CONV_FFN = 3

COL_Q = 0
COL_K = QK_WIDTH
COL_V = 2 * QK_WIDTH
COL_Z = COL_V + V_WIDTH
COL_NARROW = COL_Z + V_WIDTH
N_NARROW = 2 * N_V_HEADS
COL_U = COL_Z + V_WIDTH
COL_VB = COL_U + WIDTH_B
COL_GA = COL_VB + WIDTH_B
N_MAIN = COL_GA + 2 * 2048
LANES = 128
HALO = 8

V7X_VMEM_LIMIT_BYTES = 52 * 1024 * 1024


def _params(sem):
    return pltpu.CompilerParams(dimension_semantics=sem, vmem_limit_bytes=V7X_VMEM_LIMIT_BYTES)


def _row_tile(s, target):
    t = min(target, s)
    while s % t:
        t //= 2
    return t


def _col_tile(n, target):
    t = min(target, n)
    while n % t or t % LANES:
        t -= LANES
    return t


def _rms_rows_to(x_ref, gain_ref, h_ref, rows=32):
    tm = x_ref.shape[0]

    def body(r, carry):
        sl = pl.ds(pl.multiple_of(r * rows, rows), rows)
        x = x_ref[sl, :]
        ms = jnp.mean(x * x, axis=-1, keepdims=True)
        h_ref[sl, :] = ((x * lax.rsqrt(ms + EPS)) * gain_ref[...]).astype(h_ref.dtype)
        return carry

    lax.fori_loop(0, tm // rows, body, 0, unroll=2)


def _inproj_kernel(x_ref, gain_ref, w_ref, wn_ref, out_ref, nar_ref, h_ref):
    @pl.when(pl.program_id(1) == 0)
    def _():
        _rms_rows_to(x_ref, gain_ref, h_ref)
        nar_ref[...] = jnp.dot(h_ref[...], wn_ref[...], preferred_element_type=F32)

    out_ref[...] = jnp.dot(h_ref[...], w_ref[...], preferred_element_type=F32).astype(out_ref.dtype)


def _inproj(xf, gain, w_main, w_nar, tm, tn):
    m, d = xf.shape
    n = w_main.shape[1]
    return pl.pallas_call(
        _inproj_kernel,
        grid=(m // tm, n // tn),
        in_specs=[
            pl.BlockSpec((tm, d), lambda i, j: (i, 0)),
            pl.BlockSpec((1, d), lambda i, j: (0, 0)),
            pl.BlockSpec((d, tn), lambda i, j: (0, j)),
            pl.BlockSpec((d, LANES), lambda i, j: (0, 0)),
        ],
        out_specs=[
            pl.BlockSpec((tm, tn), lambda i, j: (i, j)),
            pl.BlockSpec((tm, LANES), lambda i, j: (i, 0)),
        ],
        out_shape=[jax.ShapeDtypeStruct((m, n), BF16), jax.ShapeDtypeStruct((m, LANES), F32)],
        scratch_shapes=[pltpu.VMEM((tm, d), BF16)],
        compiler_params=_params(("parallel", "arbitrary")),
        name="inproj",
    )(xf, gain.reshape(1, d), w_main, w_nar)


def _softplus(x):
    return jnp.maximum(x, 0.0) + jnp.log(1.0 + jnp.exp(-jnp.abs(x)))


def _delta_kernel(nh, group, q_ref, k_ref, v_ref, z_ref, nar_ref, cq_ref, ck_ref, cv_ref, alog_ref, dtb_ref, hg_ref,
                  o_ref, s_ref, cs_ref, lhs_ref, kt_ref, bd_ref, dc_ref, x0_ref, wq_ref, u_ref, at_ref, kd_ref, cd_ref):
    c = CHUNK_A
    hd = HEAD_DIM
    n_groups = q_ref.shape[0] // (group * c)
    hq0 = pl.program_id(1) * nh
    for ref in (s_ref, cs_ref, lhs_ref, kt_ref, bd_ref, dc_ref, x0_ref, wq_ref, u_ref, at_ref, kd_ref, cd_ref):
        ref[...] = jnp.zeros_like(ref)

    row = lax.broadcasted_iota(jnp.int32, (2 * c, 2 * c), 0)
    col = lax.broadcasted_iota(jnp.int32, (2 * c, 2 * c), 1)
    same_head = (row < c) == (col < c)
    causal = same_head & (row >= col)
    strict = same_head & (row > col)
    lane_lo = col < c
    lane_m = col - jnp.where(row < c, 0, 1)
    row_m = row - jnp.where(lane_lo, 0, 1)
    row_c = lax.broadcasted_iota(jnp.int32, (c, LANES), 0)
    neg_exp_alog = -jnp.exp(alog_ref[...])
    q_scale = HEAD_DIM ** -0.5
    heads = [(hh, h) for hh in range(nh) for h in range(2)]

    def dots(lhs_list, rhs_list):
        return [jnp.dot(a, b, preferred_element_type=F32) for a, b in zip(lhs_list, rhs_list)]

    def gates(ci):
        rows = pl.ds(pl.multiple_of(ci * c, c), c)
        nar = nar_ref[rows, :]
        sig = jax.nn.sigmoid(nar)
        gam = neg_exp_alog * _softplus(nar + dtb_ref[...])
        for sh in (1, 2, 4, 8, 16, 32):
            gam = gam + jnp.where(row_c >= sh, pltpu.roll(gam, sh, axis=0), 0.0)
        sig2 = jnp.concatenate([sig, sig], axis=0)
        gam2 = jnp.concatenate([gam, gam], axis=0)
        return sig2, gam2, gam2.T

    def conv_group(c0, hh):
        n = group * c
        rows = pl.ds(pl.multiple_of(c0 * c, c), n)
        srcs = ((q_ref, cq_ref, hh * hd), (k_ref, ck_ref, hh * hd),
                (v_ref, cv_ref, hh * 2 * hd), (v_ref, cv_ref, hh * 2 * hd + hd))
        outs = []
        for s, (x_ref, w_ref, c_lo) in enumerate(srcs):
            cols = slice(c_lo, c_lo + hd)
            cs_ref[hh, s, HALO:HALO + n, :] = x_ref[rows, cols].astype(F32)
            per_chunk = []
            for g in range(group):
                acc = w_ref[CONV_A - 1:CONV_A, cols] * cs_ref[hh, s, HALO + g * c:HALO + (g + 1) * c, :]
                for j in range(CONV_A - 1):
                    off = HALO - (CONV_A - 1) + j + g * c
                    acc = acc + w_ref[j:j + 1, cols] * cs_ref[hh, s, pl.ds(off, c, stride=1), :]
                per_chunk.append(jax.nn.silu(acc))
            outs.append(per_chunk)
            cs_ref[hh, s, 0:HALO, :] = cs_ref[hh, s, n:n + HALO, :]
        return outs

    def prep(grp, par, ring):
        c0 = grp * group
        conv = [conv_group(c0, hh) for hh in range(nh)]
        for g in range(group):
            sig2, gam2, gam2t = gates(c0 + g)
            for hh in range(nh):
                hq2 = 2 * (hq0 + hh)
                qc, kc, v_lo, v_hi = (conv[hh][s][g] for s in range(4))
                qn = qc * (lax.rsqrt(jnp.sum(qc * qc, axis=-1, keepdims=True) + EPS) * q_scale)
                kn = kc * lax.rsqrt(jnp.sum(kc * kc, axis=-1, keepdims=True) + EPS)
                beta_c = jnp.sum(jnp.where(lane_m == hq2, sig2, 0.0), axis=1, keepdims=True)
                gam_c = jnp.sum(jnp.where(lane_m == hq2 + N_V_HEADS, gam2, 0.0), axis=1, keepdims=True)
                gam_r = jnp.sum(jnp.where(row_m == hq2 + N_V_HEADS, gam2t, 0.0), axis=0, keepdims=True)
                gam_cb = jnp.broadcast_to(gam_c, (2 * c, 2 * c))
                decay = jnp.where(causal, jnp.exp(jnp.where(causal, gam_cb - gam_r, 0.0)), 0.0)
                egam_c = jnp.exp(gam_c)
                glast_r = jnp.where(lane_lo[0:1, :], gam_cb[c - 1:c, :], gam_cb[2 * c - 1:2 * c, :])
                kn2 = jnp.concatenate([kn, kn], axis=0)
                kt2 = kn2.T
                kdec = kt2 * jnp.exp(glast_r - gam_r)
                qd2 = jnp.concatenate([qn, qn], axis=0) * egam_c
                slot = ring * group + g
                for h in range(2):
                    keep = lane_lo if h == 0 else jnp.logical_not(lane_lo)
                    kd_ref[slot, hh, h] = jnp.where(keep, kdec, 0.0).astype(BF16)
                    cd = jnp.exp(gam_cb[(h + 1) * c - 1:(h + 1) * c, :])
                    cd_ref[slot, hh, h] = jnp.broadcast_to(cd, (HALO, LANES))
                    wq_ref[slot, hh, h, c:2 * c, :] = qd2[h * c:(h + 1) * c].astype(BF16)
                v2 = jnp.concatenate([v_lo, v_hi], axis=0)
                lhs_ref[par, g, hh] = jnp.concatenate([qn, kn], axis=0).astype(BF16)
                kt_ref[par, g, hh] = kt2.astype(BF16)
                dc_ref[par, g, hh] = decay
                bd_ref[par, g, hh] = jnp.where(strict, beta_c * decay, 0.0)
                x0_ref[par, g, hh] = jnp.concatenate([v2 * beta_c, kn2 * (beta_c * egam_c)], axis=1)

    def solve(par, ring):
        items = [(g, hh) for g in range(group) for hh in range(nh)]
        rs = dots([lhs_ref[par, g, hh] for g, hh in items], [kt_ref[par, g, hh] for g, hh in items])
        yield
        pws, xs = [], []
        for (g, hh), r in zip(items, rs):
            qk2 = jnp.concatenate([r[:c], r[:c]], axis=0)
            kk2 = jnp.concatenate([r[c:], r[c:]], axis=0)
            at_ref[ring * group + g, hh] = (qk2 * dc_ref[par, g, hh]).astype(BF16)
            pws.append((-(kk2 * bd_ref[par, g, hh])).astype(BF16))
            xs.append(x0_ref[par, g, hh])
        for level in range(6):
            if level:
                pws = [p.astype(BF16) for p in dots(pws, pws)]
            xs = [x + d for x, d in zip(xs, dots(pws, [x.astype(BF16) for x in xs]))]
            yield
        for (g, hh), x in zip(items, xs):
            slot = ring * group + g
            u_ref[slot, hh] = x[:, :hd]
            for h in range(2):
                wq_ref[slot, hh, h, 0:c, :] = x[h * c:(h + 1) * c, hd:].astype(BF16)

    def scan(grp, ring):
        for g in range(group):
            slot = ring * group + g
            rows = pl.ds(pl.multiple_of((grp * group + g) * c, c), c)
            res = dots([wq_ref[slot, hh, h] for hh, h in heads], [s_ref[hh, h].astype(BF16) for hh, h in heads])
            yield
            vnew_b, o_state = [], []
            for hh in range(nh):
                u = u_ref[slot, hh]
                r0, r1 = res[2 * hh], res[2 * hh + 1]
                vnew_b.append(jnp.concatenate([u[:c] - r0[:c], u[c:] - r1[:c]], axis=0).astype(BF16))
                o_state.append(jnp.concatenate([r0[c:], r1[c:]], axis=0))
            upd = dots([kd_ref[slot, hh, h] for hh, h in heads], [vnew_b[hh] for hh, h in heads])
            o_intra = dots([at_ref[slot, hh] for hh in range(nh)], vnew_b)
            yield
            for (hh, h), d in zip(heads, upd):
                s_ref[hh, h] = s_ref[hh, h] * cd_ref[slot, hh, h][0:1, :] + d
            for hh in range(nh):
                o2 = o_state[hh] + o_intra[hh]
                zc = z_ref[rows, hh * 2 * hd:(hh + 1) * 2 * hd].astype(F32)
                z2 = jnp.concatenate([zc[:, :hd], zc[:, hd:]], axis=0)
                y = (o2 * lax.rsqrt(jnp.mean(o2 * o2, axis=-1, keepdims=True) + EPS)) * hg_ref[...]
                y = y * jax.nn.silu(z2)
                o_ref[rows, hh * 2 * hd:hh * 2 * hd + hd] = y[:c].astype(o_ref.dtype)
                o_ref[rows, hh * 2 * hd + hd:(hh + 1) * 2 * hd] = y[c:].astype(o_ref.dtype)

    def body(t, carry):
        last = n_groups - 1
        scan_it = scan(jnp.clip(t - 2, 0, last), lax.rem(t + 1, 3))
        solve_it = solve(lax.rem(t + 1, 2), lax.rem(t + 2, 3))
        for turn in range(2 * group + 7):
            if turn % 2 == 0 or turn >= 4 * group:
                next(solve_it, None)
            else:
                next(scan_it, None)
        for _ in scan_it:
            pass
        for _ in solve_it:
            pass
        prep(jnp.minimum(t, last), lax.rem(t, 2), lax.rem(t, 3))
        return carry

    lax.fori_loop(0, n_groups + 2, body, 0)


def _delta(proj, nar, conv_w, a_log, dt_bias, head_gain, batch, seq, nh=2, group=2):
    m = proj.shape[0]
    hd = HEAD_DIM
    c2 = 2 * CHUNK_A
    assert N_QK_HEADS % nh == 0 and seq % (group * CHUNK_A) == 0
    pad = lambda v: jnp.zeros((1, LANES), F32).at[0, N_V_HEADS:2 * N_V_HEADS].set(v.astype(F32))
    qw, vw = nh * hd, nh * 2 * hd
    qb, kb, vb, zb = COL_Q // qw, COL_K // qw, COL_V // vw, COL_Z // vw
    ring = 3 * group
    return pl.pallas_call(
        functools.partial(_delta_kernel, nh, group),
        grid=(batch, N_QK_HEADS // nh),
        in_specs=[
            pl.BlockSpec((seq, qw), lambda b, h: (b, qb + h)),
            pl.BlockSpec((seq, qw), lambda b, h: (b, kb + h)),
            pl.BlockSpec((seq, vw), lambda b, h: (b, vb + h)),
            pl.BlockSpec((seq, vw), lambda b, h: (b, zb + h)),
            pl.BlockSpec((seq, LANES), lambda b, h: (b, 0)),
            pl.BlockSpec((CONV_A, qw), lambda b, h: (0, qb + h)),
            pl.BlockSpec((CONV_A, qw), lambda b, h: (0, kb + h)),
            pl.BlockSpec((CONV_A, vw), lambda b, h: (0, vb + h)),
            pl.BlockSpec((1, LANES), lambda b, h: (0, 0)),
            pl.BlockSpec((1, LANES), lambda b, h: (0, 0)),
            pl.BlockSpec((1, hd), lambda b, h: (0, 0)),
        ],
        out_specs=pl.BlockSpec((seq, vw), lambda b, h: (b, h)),
        out_shape=jax.ShapeDtypeStruct((m, V_WIDTH), BF16),
        scratch_shapes=[
            pltpu.VMEM((nh, 2, hd, hd), F32),
            pltpu.VMEM((nh, 4, 2 * HALO + group * CHUNK_A, LANES), F32),
            pltpu.VMEM((2, group, nh, c2, hd), BF16),
            pltpu.VMEM((2, group, nh, hd, c2), BF16),
            pltpu.VMEM((2, group, nh, c2, c2), F32),
            pltpu.VMEM((2, group, nh, c2, c2), F32),
            pltpu.VMEM((2, group, nh, c2, 2 * hd), F32),
            pltpu.VMEM((ring, nh, 2, c2, hd), BF16),
            pltpu.VMEM((ring, nh, c2, hd), F32),
            pltpu.VMEM((ring, nh, c2, c2), BF16),
            pltpu.VMEM((ring, nh, 2, hd, c2), BF16),
            pltpu.VMEM((ring, nh, 2, HALO, LANES), F32),
        ],
        compiler_params=_params(("parallel", "parallel")),
        name="delta",
    )(proj, proj, proj, proj, nar, conv_w, conv_w, conv_w, pad(a_log), pad(dt_bias), head_gain.reshape(1, hd))


def _sgu_kernel(u_ref, v_ref, gain_ref, w_ref, b_ref, o_ref):
    t = CHUNK_B
    g_dim = GROUP_DIM_B
    row = lax.broadcasted_iota(jnp.int32, (t, t), 0)
    col = lax.broadcasted_iota(jnp.int32, (t, t), 1)
    keep = row >= col

    def chunk(ci, carry):
        rows = pl.ds(pl.multiple_of(ci * t, t), t)
        v = jax.nn.gelu(v_ref[rows, :].astype(F32))
        ms = jnp.mean(v * v, axis=-1, keepdims=True)
        vn = ((v * lax.rsqrt(ms + EPS)) * gain_ref[...]).astype(BF16)
        for g in range(N_GROUPS_B):
            cols = slice(g * g_dim, (g + 1) * g_dim)
            wm = jnp.where(keep, w_ref[g], 0.0).astype(BF16)
            mixed = jnp.dot(wm, vn[:, cols], preferred_element_type=F32) + b_ref[g]
            u = jax.nn.gelu(u_ref[rows, cols].astype(F32))
            o_ref[rows, cols] = (u * mixed).astype(o_ref.dtype)
        return carry

    lax.fori_loop(0, u_ref.shape[0] // t, chunk, 0)


def _sgu(proj, gain, w_s, b_s, tr):
    m = proj.shape[0]
    wb = WIDTH_B
    b_bc = jnp.broadcast_to(b_s.astype(F32)[:, :, None], (N_GROUPS_B, CHUNK_B, GROUP_DIM_B))
    return pl.pallas_call(
        _sgu_kernel,
        grid=(m // tr,),
        in_specs=[
            pl.BlockSpec((tr, wb), lambda i: (i, COL_U // wb)),
            pl.BlockSpec((tr, wb), lambda i: (i, COL_VB // wb)),
            pl.BlockSpec((1, wb), lambda i: (0, 0)),
            pl.BlockSpec((N_GROUPS_B, CHUNK_B, CHUNK_B), lambda i: (0, 0, 0)),
            pl.BlockSpec((N_GROUPS_B, CHUNK_B, GROUP_DIM_B), lambda i: (0, 0, 0)),
        ],
        out_specs=pl.BlockSpec((tr, wb), lambda i: (i, 0)),
        out_shape=jax.ShapeDtypeStruct((m, wb), BF16),
        compiler_params=_params(("parallel",)),
        name="sgu",
    )(proj, proj, gain.reshape(1, wb), w_s, b_bc)


def _merge_kernel(o_ref, s_ref, ga_ref, gb_ref, wa_ref, wb_ref, out_ref):
    tm = out_ref.shape[0]
    hb = tm // 2
    for r in range(2):
        rows = slice(r * hb, (r + 1) * hb)
        ya = jnp.dot(o_ref[rows, :], wa_ref[...], preferred_element_type=F32)
        yb = jnp.dot(s_ref[rows, :], wb_ref[...], preferred_element_type=F32)
        ga = jax.nn.sigmoid(ga_ref[rows, :].astype(F32))
        gb = jax.nn.sigmoid(gb_ref[rows, :].astype(F32))
        out_ref[rows, :] = (ga * ya + gb * yb).astype(out_ref.dtype)


def _merge(o, s, proj, wa, wb, tm, tn):
    m = o.shape[0]
    d = wa.shape[1]
    ga0 = COL_GA // tn
    gb0 = (COL_GA + d) // tn
    return pl.pallas_call(
        _merge_kernel,
        grid=(m // tm, d // tn),
        in_specs=[
            pl.BlockSpec((tm, o.shape[1]), lambda i, j: (i, 0)),
            pl.BlockSpec((tm, s.shape[1]), lambda i, j: (i, 0)),
            pl.BlockSpec((tm, tn), lambda i, j: (i, ga0 + j)),
            pl.BlockSpec((tm, tn), lambda i, j: (i, gb0 + j)),
            pl.BlockSpec((wa.shape[0], tn), lambda i, j: (0, j)),
            pl.BlockSpec((wb.shape[0], tn), lambda i, j: (0, j)),
        ],
        out_specs=pl.BlockSpec((tm, tn), lambda i, j: (i, j)),
        out_shape=jax.ShapeDtypeStruct((m, d), BF16),
        compiler_params=_params(("parallel", "arbitrary")),
        name="merge",
    )(o, s, proj, proj, wa, wb)


def _mm_res_kernel(a_ref, w_ref, res_ref, out_ref):
    tm = out_ref.shape[0]
    hb = tm // 2
    for r in range(2):
        rows = slice(r * hb, (r + 1) * hb)
        out_ref[rows, :] = res_ref[rows, :] + jnp.dot(a_ref[rows, :], w_ref[...], preferred_element_type=F32)


def _resident(block_shape, index_map, n_blocks):
    if n_blocks == 1:
        return pl.BlockSpec(block_shape, index_map, pipeline_mode=pl.Buffered(1))
    return pl.BlockSpec(block_shape, index_map)


def _mm_res(a, w, res, tm, tn, name):
    m, k = a.shape
    n = w.shape[1]
    return pl.pallas_call(
        _mm_res_kernel,
        grid=(m // tm, n // tn),
        in_specs=[
            pl.BlockSpec((tm, k), lambda i, j: (i, 0)),
            _resident((k, tn), lambda i, j: (0, j), n // tn),
            pl.BlockSpec((tm, tn), lambda i, j: (i, j)),
        ],
        out_specs=pl.BlockSpec((tm, tn), lambda i, j: (i, j)),
        out_shape=jax.ShapeDtypeStruct((m, n), F32),
        compiler_params=_params(("parallel", "arbitrary")),
        name=name,
    )(a, w, res)


def _ffn_up_kernel(blocks_per_seq, x_ref, gain_ref, wg_ref, wv_ref, cg_ref, cv_ref, bg_ref, bv_ref, out_ref,
                   h_ref, ug_ref, uv_ref, tg_ref, tv_ref):
    i = pl.program_id(0)
    j = pl.program_id(1)
    tm, tn = out_ref.shape
    half = min(tn, 2 * LANES)
    sub = 128

    @pl.when(j == 0)
    def _():
        _rms_rows_to(x_ref, gain_ref, h_ref)

    first = (i % blocks_per_seq) == 0
    h = h_ref[...]
    for c0 in range(0, tn, half):
        slabs = range(c0 // LANES, (c0 + half) // LANES)
        for w_ref, u_ref, t_ref in ((wg_ref, ug_ref, tg_ref), (wv_ref, uv_ref, tv_ref)):
            res = jnp.dot(h, w_ref[:, c0:c0 + half], preferred_element_type=F32)
            for s in slabs:
                u_ref[s, 0:HALO, :] = jnp.where(first, 0.0, t_ref[j, s])
                u_ref[s, HALO:, :] = res[:, s * LANES - c0:(s + 1) * LANES - c0]
                t_ref[j, s] = u_ref[s, tm:tm + HALO, :]

        def conv(u_ref, c_ref, b_ref, s, r):
            cols = slice(s * LANES, (s + 1) * LANES)
            acc = b_ref[:, cols] + c_ref[CONV_FFN - 1:CONV_FFN, cols] * u_ref[s, HALO + r * sub:HALO + (r + 1) * sub, :]
            for t in range(CONV_FFN - 1):
                off = HALO - (CONV_FFN - 1) + t + r * sub
                acc = acc + c_ref[t:t + 1, cols] * u_ref[s, pl.ds(off, sub, stride=1), :]
            return acc

        for s in slabs:
            for r in range(tm // sub):
                gate = conv(ug_ref, cg_ref, bg_ref, s, r)
                val = conv(uv_ref, cv_ref, bv_ref, s, r)
                out_ref[r * sub:(r + 1) * sub, s * LANES:(s + 1) * LANES] = (jax.nn.silu(gate) * val).astype(out_ref.dtype)


def _ffn_up(xf, gain, w_up, conv_w, conv_b, seq, tm, tn):
    m, d = xf.shape
    nj = D_FF // tn
    b2 = conv_b.reshape(1, 2 * D_FF).astype(F32)
    return pl.pallas_call(
        functools.partial(_ffn_up_kernel, seq // tm),
        grid=(m // tm, nj),
        in_specs=[
            pl.BlockSpec((tm, d), lambda i, j: (i, 0)),
            pl.BlockSpec((1, d), lambda i, j: (0, 0)),
            pl.BlockSpec((d, tn), lambda i, j: (0, j)),
            pl.BlockSpec((d, tn), lambda i, j: (0, nj + j)),
            pl.BlockSpec((CONV_FFN, tn), lambda i, j: (0, j)),
            pl.BlockSpec((CONV_FFN, tn), lambda i, j: (0, nj + j)),
            pl.BlockSpec((1, tn), lambda i, j: (0, j)),
            pl.BlockSpec((1, tn), lambda i, j: (0, nj + j)),
        ],
        out_specs=pl.BlockSpec((tm, tn), lambda i, j: (i, j)),
        out_shape=jax.ShapeDtypeStruct((m, D_FF), BF16),
        scratch_shapes=[
            pltpu.VMEM((tm, d), BF16),
            pltpu.VMEM((tn // LANES, tm + HALO, LANES), F32),
            pltpu.VMEM((tn // LANES, tm + HALO, LANES), F32),
            pltpu.VMEM((nj, tn // LANES, HALO, LANES), F32),
            pltpu.VMEM((nj, tn // LANES, HALO, LANES), F32),
        ],
        compiler_params=_params(("arbitrary", "arbitrary")),
        name="ffn_up",
    )(xf, gain.reshape(1, d), w_up, w_up, conv_w, conv_w, b2, b2)


def _ple_kernel(last, x_ref, gain_ref, wg_ref, p_ref, wp_ref, fgain_ref, out_ref, h_ref):
    tm = out_ref.shape[0]
    hb = tm // 2
    _rms_rows_to(x_ref, gain_ref, h_ref)
    for r in range(2):
        rows = slice(r * hb, (r + 1) * hb)
        gate = jax.nn.sigmoid(jnp.dot(h_ref[rows, :], wg_ref[...], preferred_element_type=F32))
        emb = jnp.dot(p_ref[rows, :].astype(BF16), wp_ref[...], preferred_element_type=F32)
        y = x_ref[rows, :] + gate * emb
        if last:
            y = (y * lax.rsqrt(jnp.mean(y * y, axis=-1, keepdims=True) + EPS)) * fgain_ref[...]
        out_ref[rows, :] = y


def _ple(xf, gain, wg, p3, layer, wp, final_gain, last, tm):
    m, d = xf.shape
    pd = p3.shape[2]
    return pl.pallas_call(
        functools.partial(_ple_kernel, last),
        grid=(m // tm,),
        in_specs=[
            pl.BlockSpec((tm, d), lambda i: (i, 0)),
            pl.BlockSpec((1, d), lambda i: (0, 0)),
            _resident((d, d), lambda i: (0, 0), 1),
            pl.BlockSpec((None, tm, pd), lambda i: (layer, i, 0)),
            _resident((pd, d), lambda i: (0, 0), 1),
            pl.BlockSpec((1, d), lambda i: (0, 0)),
        ],
        out_specs=pl.BlockSpec((tm, d), lambda i: (i, 0)),
        out_shape=jax.ShapeDtypeStruct((m, d), F32),
        scratch_shapes=[pltpu.VMEM((tm, d), BF16)],
        compiler_params=_params(("parallel",)),
        name="ple",
    )(xf, gain.reshape(1, d), wg, p3, wp, final_gain.reshape(1, d))


def kernel(x, p, norm_mix, w_in, conv_qkv, a_log, dt_bias, head_norm, sgu_norm, w_spatial, b_spatial, w_branch_a, w_branch_b, w_out, norm_ffn, w_ffn_up, conv_ffn, b_conv_ffn, w_ffn_down, norm_ple, w_ple_gate, w_ple_proj, norm_final):
    batch, seq, d = x.shape
    depth = w_in.shape[0]
    m = batch * seq
    assert d == V_WIDTH and seq % CHUNK_B == 0 and w_in.shape[2] == N_MAIN + N_NARROW
    tm = _row_tile(seq, 1024)
    tm_res = _row_tile(seq, 512)
    xf = x.reshape(m, d).astype(F32)
    p3 = p.reshape(depth, m, p.shape[-1])
    for i in range(depth):
        wi = w_in[i]
        w_main = jnp.concatenate([wi[:, :COL_NARROW], wi[:, COL_NARROW + N_NARROW:]], axis=1).astype(BF16)
        w_nar = jnp.pad(wi[:, COL_NARROW:COL_NARROW + N_NARROW], ((0, 0), (0, LANES - N_NARROW))).astype(BF16)
        proj, nar = _inproj(xf, norm_mix[i], w_main, w_nar, tm, _col_tile(N_MAIN, 1024))
        o = _delta(proj, nar, conv_qkv[i].astype(F32), a_log[i], dt_bias[i], head_norm[i].astype(F32), batch, seq)
        s = _sgu(proj, sgu_norm[i].astype(F32), w_spatial[i].astype(F32), b_spatial[i], _row_tile(seq, 512))
        merged = _merge(o, s, proj, w_branch_a[i].astype(BF16), w_branch_b[i].astype(BF16), tm, 512)
        xf = _mm_res(merged, w_out[i].astype(BF16), xf, tm_res, d, "out_proj")
        act = _ffn_up(xf, norm_ffn[i].astype(F32), w_ffn_up[i].astype(BF16), conv_ffn[i].astype(F32),
                      b_conv_ffn[i], seq, tm, _col_tile(D_FF, 512))
        xf = _mm_res(act, w_ffn_down[i].astype(BF16), xf, tm, 512, "ffn_down")
        xf = _ple(xf, norm_ple[i].astype(F32), w_ple_gate[i].astype(BF16), p3, i, w_ple_proj[i].astype(BF16),
                  norm_final.astype(F32), i == depth - 1, tm_res)
    return xf.reshape(batch, seq, d).astype(x.dtype)
```

```python
import functools

import jax
import jax.numpy as jnp
from jax import lax
from jax.experimental import pallas as pl
from jax.experimental.pallas import tpu as pltpu

F32 = jnp.float32
BF16 = jnp.bfloat16
EPS = 1e-6

HEAD_DIM = 128
N_QK_HEADS = 8
N_V_HEADS = 16
QK_WIDTH = N_QK_HEADS * HEAD_DIM
V_WIDTH = N_V_HEADS * HEAD_DIM
CONV_A = 4
CHUNK_A = 64
N_GROUPS_B = 8
GROUP_DIM_B = 128
WIDTH_B = N_GROUPS_B * GROUP_DIM_B
CHUNK_B = 128
D_FF = 5632
CONV_FFN = 3

COL_Q = 0
COL_K = QK_WIDTH
COL_V = 2 * QK_WIDTH
COL_Z = COL_V + V_WIDTH
COL_NARROW = COL_Z + V_WIDTH
N_NARROW = 2 * N_V_HEADS
COL_U = COL_Z + V_WIDTH
COL_VB = COL_U + WIDTH_B
COL_GA = COL_VB + WIDTH_B
N_MAIN = COL_GA + 2 * 2048
LANES = 128
HALO = 8

V7X_VMEM_LIMIT_BYTES = 52 * 1024 * 1024


def _params(sem):
    return pltpu.CompilerParams(dimension_semantics=sem, vmem_limit_bytes=V7X_VMEM_LIMIT_BYTES)


def _row_tile(s, target):
    t = min(target, s)
    while s % t:
        t //= 2
    return t


def _col_tile(n, target):
    t = min(target, n)
    while n % t or t % LANES:
        t -= LANES
    return t


def _rms_rows_to(x_ref, gain_ref, h_ref, rows=32):
    tm = x_ref.shape[0]

    def body(r, carry):
        sl = pl.ds(pl.multiple_of(r * rows, rows), rows)
        x = x_ref[sl, :]
        ms = jnp.mean(x * x, axis=-1, keepdims=True)
        h_ref[sl, :] = ((x * lax.rsqrt(ms + EPS)) * gain_ref[...]).astype(h_ref.dtype)
        return carry

    lax.fori_loop(0, tm // rows, body, 0, unroll=2)


def _repack_kernel(w_ref, main_ref, nar_ref):
    x = w_ref[...]
    rows = x.shape[0]
    main_ref[:, :COL_NARROW] = x[:, :COL_NARROW].astype(BF16)
    main_ref[:, COL_NARROW:] = x[:, COL_NARROW + N_NARROW:].astype(BF16)
    nar = x[:, COL_NARROW:COL_NARROW + N_NARROW]
    nar_ref[...] = jnp.concatenate([nar, jnp.zeros((rows, LANES - N_NARROW), F32)], axis=1).astype(BF16)


def _repack_w_in(w_in, layer, tr=64):
    _, d, n = w_in.shape
    return pl.pallas_call(
        _repack_kernel,
        grid=(d // tr,),
        in_specs=[pl.BlockSpec((None, tr, n), lambda i: (layer, i, 0))],
        out_specs=[pl.BlockSpec((tr, N_MAIN), lambda i: (i, 0)), pl.BlockSpec((tr, LANES), lambda i: (i, 0))],
        out_shape=[jax.ShapeDtypeStruct((d, N_MAIN), BF16), jax.ShapeDtypeStruct((d, LANES), BF16)],
        compiler_params=_params(("parallel",)),
        name="repack_w_in",
    )(w_in)


def _inproj_kernel(x_ref, gain_ref, w_ref, wn_ref, out_ref, nar_ref, h_ref):
    @pl.when(pl.program_id(1) == 0)
    def _():
        _rms_rows_to(x_ref, gain_ref, h_ref)
        nar_ref[...] = jnp.dot(h_ref[...], wn_ref[...], preferred_element_type=F32)

    out_ref[...] = jnp.dot(h_ref[...], w_ref[...], preferred_element_type=F32).astype(out_ref.dtype)


def _inproj(xf, gain, w_main, w_nar, tm, tn):
    m, d = xf.shape
    n = w_main.shape[1]
    return pl.pallas_call(
        _inproj_kernel,
        grid=(m // tm, n // tn),
        in_specs=[
            pl.BlockSpec((tm, d), lambda i, j: (i, 0)),
            pl.BlockSpec((1, d), lambda i, j: (0, 0)),
            pl.BlockSpec((d, tn), lambda i, j: (0, j)),
            pl.BlockSpec((d, LANES), lambda i, j: (0, 0)),
        ],
        out_specs=[
            pl.BlockSpec((tm, tn), lambda i, j: (i, j)),
            pl.BlockSpec((tm, LANES), lambda i, j: (i, 0)),
        ],
        out_shape=[jax.ShapeDtypeStruct((m, n), BF16), jax.ShapeDtypeStruct((m, LANES), F32)],
        scratch_shapes=[pltpu.VMEM((tm, d), BF16)],
        compiler_params=_params(("parallel", "arbitrary")),
        name="inproj",
    )(xf, gain.reshape(1, d), w_main, w_nar)


def _softplus(x):
    return jnp.maximum(x, 0.0) + jnp.log(1.0 + jnp.exp(-jnp.abs(x)))


def _delta_kernel(nh, group, q_ref, k_ref, v_ref, z_ref, nar_ref, cq_ref, ck_ref, cv_ref, alog_ref, dtb_ref, hg_ref,
                  o_ref, s_ref, cs_ref, lhs_ref, kt_ref, bd_ref, dc_ref, x0_ref, wq_ref, u_ref, at_ref, kd_ref, cd_ref):
    c = CHUNK_A
    hd = HEAD_DIM
    n_groups = q_ref.shape[0] // (group * c)
    hq0 = pl.program_id(1) * nh
    for ref in (s_ref, cs_ref, lhs_ref, kt_ref, bd_ref, dc_ref, x0_ref, wq_ref, u_ref, at_ref, kd_ref, cd_ref):
        ref[...] = jnp.zeros_like(ref)

    row = lax.broadcasted_iota(jnp.int32, (2 * c, 2 * c), 0)
    col = lax.broadcasted_iota(jnp.int32, (2 * c, 2 * c), 1)
    same_head = (row < c) == (col < c)
    causal = same_head & (row >= col)
    strict = same_head & (row > col)
    lane_lo = col < c
    lane_m = col - jnp.where(row < c, 0, 1)
    row_m = row - jnp.where(lane_lo, 0, 1)
    row_c = lax.broadcasted_iota(jnp.int32, (c, LANES), 0)
    neg_exp_alog = -jnp.exp(alog_ref[...])
    q_scale = HEAD_DIM ** -0.5
    heads = [(hh, h) for hh in range(nh) for h in range(2)]

    def dots(lhs_list, rhs_list):
        return [jnp.dot(a, b, preferred_element_type=F32) for a, b in zip(lhs_list, rhs_list)]

    def gates(ci):
        rows = pl.ds(pl.multiple_of(ci * c, c), c)
        nar = nar_ref[rows, :]
        sig = jax.nn.sigmoid(nar)
        gam = neg_exp_alog * _softplus(nar + dtb_ref[...])
        for sh in (1, 2, 4, 8, 16, 32):
            gam = gam + jnp.where(row_c >= sh, pltpu.roll(gam, sh, axis=0), 0.0)
        sig2 = jnp.concatenate([sig, sig], axis=0)
        gam2 = jnp.concatenate([gam, gam], axis=0)
        return sig2, gam2, gam2.T

    def conv_group(c0, hh):
        n = group * c
        rows = pl.ds(pl.multiple_of(c0 * c, c), n)
        srcs = ((q_ref, cq_ref, hh * hd), (k_ref, ck_ref, hh * hd),
                (v_ref, cv_ref, hh * 2 * hd), (v_ref, cv_ref, hh * 2 * hd + hd))
        outs = []
        for s, (x_ref, w_ref, c_lo) in enumerate(srcs):
            cols = slice(c_lo, c_lo + hd)
            cs_ref[hh, s, HALO:HALO + n, :] = x_ref[rows, cols].astype(F32)
            per_chunk = []
            for g in range(group):
                acc = w_ref[CONV_A - 1:CONV_A, cols] * cs_ref[hh, s, HALO + g * c:HALO + (g + 1) * c, :]
                for j in range(CONV_A - 1):
                    off = HALO - (CONV_A - 1) + j + g * c
                    acc = acc + w_ref[j:j + 1, cols] * cs_ref[hh, s, pl.ds(off, c, stride=1), :]
                per_chunk.append(jax.nn.silu(acc))
            outs.append(per_chunk)
            cs_ref[hh, s, 0:HALO, :] = cs_ref[hh, s, n:n + HALO, :]
        return outs

    def prep(grp, par, ring):
        c0 = grp * group
        conv = [conv_group(c0, hh) for hh in range(nh)]
        for g in range(group):
            sig2, gam2, gam2t = gates(c0 + g)
            for hh in range(nh):
                hq2 = 2 * (hq0 + hh)
                qc, kc, v_lo, v_hi = (conv[hh][s][g] for s in range(4))
                qn = qc * (lax.rsqrt(jnp.sum(qc * qc, axis=-1, keepdims=True) + EPS) * q_scale)
                kn = kc * lax.rsqrt(jnp.sum(kc * kc, axis=-1, keepdims=True) + EPS)
                beta_c = jnp.sum(jnp.where(lane_m == hq2, sig2, 0.0), axis=1, keepdims=True)
                gam_c = jnp.sum(jnp.where(lane_m == hq2 + N_V_HEADS, gam2, 0.0), axis=1, keepdims=True)
                gam_r = jnp.sum(jnp.where(row_m == hq2 + N_V_HEADS, gam2t, 0.0), axis=0, keepdims=True)
                gam_cb = jnp.broadcast_to(gam_c, (2 * c, 2 * c))
                decay = jnp.where(causal, jnp.exp(jnp.where(causal, gam_cb - gam_r, 0.0)), 0.0)
                egam_c = jnp.exp(gam_c)
                glast_r = jnp.where(lane_lo[0:1, :], gam_cb[c - 1:c, :], gam_cb[2 * c - 1:2 * c, :])
                kn2 = jnp.concatenate([kn, kn], axis=0)
                kt2 = kn2.T
                kdec = kt2 * jnp.exp(glast_r - gam_r)
                qd2 = jnp.concatenate([qn, qn], axis=0) * egam_c
                slot = ring * group + g
                for h in range(2):
                    keep = lane_lo if h == 0 else jnp.logical_not(lane_lo)
                    kd_ref[slot, hh, h] = jnp.where(keep, kdec, 0.0).astype(BF16)
                    cd = jnp.exp(gam_cb[(h + 1) * c - 1:(h + 1) * c, :])
                    cd_ref[slot, hh, h] = jnp.broadcast_to(cd, (HALO, LANES))
                    wq_ref[slot, hh, h, c:2 * c, :] = qd2[h * c:(h + 1) * c].astype(BF16)
                v2 = jnp.concatenate([v_lo, v_hi], axis=0)
                lhs_ref[par, g, hh] = jnp.concatenate([qn, kn], axis=0).astype(BF16)
                kt_ref[par, g, hh] = kt2.astype(BF16)
                dc_ref[par, g, hh] = decay
                bd_ref[par, g, hh] = jnp.where(strict, beta_c * decay, 0.0)
                x0_ref[par, g, hh] = jnp.concatenate([v2 * beta_c, kn2 * (beta_c * egam_c)], axis=1)

    def solve(par, ring):
        items = [(g, hh) for g in range(group) for hh in range(nh)]
        rs = dots([lhs_ref[par, g, hh] for g, hh in items], [kt_ref[par, g, hh] for g, hh in items])
        yield
        pws, xs = [], []
        for (g, hh), r in zip(items, rs):
            qk2 = jnp.concatenate([r[:c], r[:c]], axis=0)
            kk2 = jnp.concatenate([r[c:], r[c:]], axis=0)
            at_ref[ring * group + g, hh] = (qk2 * dc_ref[par, g, hh]).astype(BF16)
            pws.append((-(kk2 * bd_ref[par, g, hh])).astype(BF16))
            xs.append(x0_ref[par, g, hh])
        for level in range(6):
            if level:
                pws = [p.astype(BF16) for p in dots(pws, pws)]
            xs = [x + d for x, d in zip(xs, dots(pws, [x.astype(BF16) for x in xs]))]
            yield
        for (g, hh), x in zip(items, xs):
            slot = ring * group + g
            u_ref[slot, hh] = x[:, :hd]
            for h in range(2):
                wq_ref[slot, hh, h, 0:c, :] = x[h * c:(h + 1) * c, hd:].astype(BF16)

    def scan(grp, ring):
        for g in range(group):
            slot = ring * group + g
            rows = pl.ds(pl.multiple_of((grp * group + g) * c, c), c)
            res = dots([wq_ref[slot, hh, h] for hh, h in heads], [s_ref[hh, h].astype(BF16) for hh, h in heads])
            yield
            vnew_b, o_state = [], []
            for hh in range(nh):
                u = u_ref[slot, hh]
                r0, r1 = res[2 * hh], res[2 * hh + 1]
                vnew_b.append(jnp.concatenate([u[:c] - r0[:c], u[c:] - r1[:c]], axis=0).astype(BF16))
                o_state.append(jnp.concatenate([r0[c:], r1[c:]], axis=0))
            upd = dots([kd_ref[slot, hh, h] for hh, h in heads], [vnew_b[hh] for hh, h in heads])
            o_intra = dots([at_ref[slot, hh] for hh in range(nh)], vnew_b)
            yield
            for (hh, h), d in zip(heads, upd):
                s_ref[hh, h] = s_ref[hh, h] * cd_ref[slot, hh, h][0:1, :] + d
            for hh in range(nh):
                o2 = o_state[hh] + o_intra[hh]
                zc = z_ref[rows, hh * 2 * hd:(hh + 1) * 2 * hd].astype(F32)
                z2 = jnp.concatenate([zc[:, :hd], zc[:, hd:]], axis=0)
                y = (o2 * lax.rsqrt(jnp.mean(o2 * o2, axis=-1, keepdims=True) + EPS)) * hg_ref[...]
                y = y * jax.nn.silu(z2)
                o_ref[rows, hh * 2 * hd:hh * 2 * hd + hd] = y[:c].astype(o_ref.dtype)
                o_ref[rows, hh * 2 * hd + hd:(hh + 1) * 2 * hd] = y[c:].astype(o_ref.dtype)

    def body(t, carry):
        last = n_groups - 1
        scan_it = scan(jnp.clip(t - 2, 0, last), lax.rem(t + 1, 3))
        solve_it = solve(lax.rem(t + 1, 2), lax.rem(t + 2, 3))
        for turn in range(2 * group + 7):
            if turn % 2 == 0 or turn >= 4 * group:
                next(solve_it, None)
            else:
                next(scan_it, None)
        for _ in scan_it:
            pass
        for _ in solve_it:
            pass
        prep(jnp.minimum(t, last), lax.rem(t, 2), lax.rem(t, 3))
        return carry

    lax.fori_loop(0, n_groups + 2, body, 0)


def _delta(proj, nar, conv_w, a_log, dt_bias, head_gain, batch, seq, nh=2, group=2):
    m = proj.shape[0]
    hd = HEAD_DIM
    c2 = 2 * CHUNK_A
    assert N_QK_HEADS % nh == 0 and seq % (group * CHUNK_A) == 0
    pad = lambda v: jnp.zeros((1, LANES), F32).at[0, N_V_HEADS:2 * N_V_HEADS].set(v.astype(F32))
    qw, vw = nh * hd, nh * 2 * hd
    qb, kb, vb, zb = COL_Q // qw, COL_K // qw, COL_V // vw, COL_Z // vw
    ring = 3 * group
    return pl.pallas_call(
        functools.partial(_delta_kernel, nh, group),
        grid=(batch, N_QK_HEADS // nh),
        in_specs=[
            pl.BlockSpec((seq, qw), lambda b, h: (b, qb + h)),
            pl.BlockSpec((seq, qw), lambda b, h: (b, kb + h)),
            pl.BlockSpec((seq, vw), lambda b, h: (b, vb + h)),
            pl.BlockSpec((seq, vw), lambda b, h: (b, zb + h)),
            pl.BlockSpec((seq, LANES), lambda b, h: (b, 0)),
            pl.BlockSpec((CONV_A, qw), lambda b, h: (0, qb + h)),
            pl.BlockSpec((CONV_A, qw), lambda b, h: (0, kb + h)),
            pl.BlockSpec((CONV_A, vw), lambda b, h: (0, vb + h)),
            pl.BlockSpec((1, LANES), lambda b, h: (0, 0)),
            pl.BlockSpec((1, LANES), lambda b, h: (0, 0)),
            pl.BlockSpec((1, hd), lambda b, h: (0, 0)),
        ],
        out_specs=pl.BlockSpec((seq, vw), lambda b, h: (b, h)),
        out_shape=jax.ShapeDtypeStruct((m, V_WIDTH), BF16),
        scratch_shapes=[
            pltpu.VMEM((nh, 2, hd, hd), F32),
            pltpu.VMEM((nh, 4, 2 * HALO + group * CHUNK_A, LANES), F32),
            pltpu.VMEM((2, group, nh, c2, hd), BF16),
            pltpu.VMEM((2, group, nh, hd, c2), BF16),
            pltpu.VMEM((2, group, nh, c2, c2), F32),
            pltpu.VMEM((2, group, nh, c2, c2), F32),
            pltpu.VMEM((2, group, nh, c2, 2 * hd), F32),
            pltpu.VMEM((ring, nh, 2, c2, hd), BF16),
            pltpu.VMEM((ring, nh, c2, hd), F32),
            pltpu.VMEM((ring, nh, c2, c2), BF16),
            pltpu.VMEM((ring, nh, 2, hd, c2), BF16),
            pltpu.VMEM((ring, nh, 2, HALO, LANES), F32),
        ],
        compiler_params=_params(("parallel", "parallel")),
        name="delta",
    )(proj, proj, proj, proj, nar, conv_w, conv_w, conv_w, pad(a_log), pad(dt_bias), head_gain.reshape(1, hd))


def _sgu_kernel(u_ref, v_ref, gain_ref, w_ref, b_ref, o_ref):
    t = CHUNK_B
    g_dim = GROUP_DIM_B
    row = lax.broadcasted_iota(jnp.int32, (t, t), 0)
    col = lax.broadcasted_iota(jnp.int32, (t, t), 1)
    keep = row >= col

    def chunk(ci, carry):
        rows = pl.ds(pl.multiple_of(ci * t, t), t)
        v = jax.nn.gelu(v_ref[rows, :].astype(F32))
        ms = jnp.mean(v * v, axis=-1, keepdims=True)
        vn = ((v * lax.rsqrt(ms + EPS)) * gain_ref[...]).astype(BF16)
        for g in range(N_GROUPS_B):
            cols = slice(g * g_dim, (g + 1) * g_dim)
            wm = jnp.where(keep, w_ref[g], 0.0).astype(BF16)
            mixed = jnp.dot(wm, vn[:, cols], preferred_element_type=F32) + b_ref[g]
            u = jax.nn.gelu(u_ref[rows, cols].astype(F32))
            o_ref[rows, cols] = (u * mixed).astype(o_ref.dtype)
        return carry

    lax.fori_loop(0, u_ref.shape[0] // t, chunk, 0)


def _sgu(proj, gain, w_s, b_s, tr):
    m = proj.shape[0]
    wb = WIDTH_B
    b_bc = jnp.broadcast_to(b_s.astype(F32)[:, :, None], (N_GROUPS_B, CHUNK_B, GROUP_DIM_B))
    return pl.pallas_call(
        _sgu_kernel,
        grid=(m // tr,),
        in_specs=[
            pl.BlockSpec((tr, wb), lambda i: (i, COL_U // wb)),
            pl.BlockSpec((tr, wb), lambda i: (i, COL_VB // wb)),
            pl.BlockSpec((1, wb), lambda i: (0, 0)),
            pl.BlockSpec((N_GROUPS_B, CHUNK_B, CHUNK_B), lambda i: (0, 0, 0)),
            pl.BlockSpec((N_GROUPS_B, CHUNK_B, GROUP_DIM_B), lambda i: (0, 0, 0)),
        ],
        out_specs=pl.BlockSpec((tr, wb), lambda i: (i, 0)),
        out_shape=jax.ShapeDtypeStruct((m, wb), BF16),
        compiler_params=_params(("parallel",)),
        name="sgu",
    )(proj, proj, gain.reshape(1, wb), w_s, b_bc)


def _merge_kernel(o_ref, s_ref, ga_ref, gb_ref, wa_ref, wb_ref, out_ref):
    tm = out_ref.shape[0]
    hb = tm // 2
    for r in range(2):
        rows = slice(r * hb, (r + 1) * hb)
        ya = jnp.dot(o_ref[rows, :], wa_ref[...], preferred_element_type=F32)
        yb = jnp.dot(s_ref[rows, :], wb_ref[...], preferred_element_type=F32)
        ga = jax.nn.sigmoid(ga_ref[rows, :].astype(F32))
        gb = jax.nn.sigmoid(gb_ref[rows, :].astype(F32))
        out_ref[rows, :] = (ga * ya + gb * yb).astype(out_ref.dtype)


def _merge(o, s, proj, wa, wb, tm, tn):
    m = o.shape[0]
    d = wa.shape[1]
    ga0 = COL_GA // tn
    gb0 = (COL_GA + d) // tn
    return pl.pallas_call(
        _merge_kernel,
        grid=(m // tm, d // tn),
        in_specs=[
            pl.BlockSpec((tm, o.shape[1]), lambda i, j: (i, 0)),
            pl.BlockSpec((tm, s.shape[1]), lambda i, j: (i, 0)),
            pl.BlockSpec((tm, tn), lambda i, j: (i, ga0 + j)),
            pl.BlockSpec((tm, tn), lambda i, j: (i, gb0 + j)),
            pl.BlockSpec((wa.shape[0], tn), lambda i, j: (0, j)),
            pl.BlockSpec((wb.shape[0], tn), lambda i, j: (0, j)),
        ],
        out_specs=pl.BlockSpec((tm, tn), lambda i, j: (i, j)),
        out_shape=jax.ShapeDtypeStruct((m, d), BF16),
        compiler_params=_params(("parallel", "arbitrary")),
        name="merge",
    )(o, s, proj, proj, wa, wb)


def _mm_res_kernel(a_ref, w_ref, res_ref, out_ref):
    tm = out_ref.shape[0]
    hb = tm // 2
    for r in range(2):
        rows = slice(r * hb, (r + 1) * hb)
        out_ref[rows, :] = res_ref[rows, :] + jnp.dot(a_ref[rows, :], w_ref[...], preferred_element_type=F32)


def _resident(block_shape, index_map, n_blocks):
    if n_blocks == 1:
        return pl.BlockSpec(block_shape, index_map, pipeline_mode=pl.Buffered(1))
    return pl.BlockSpec(block_shape, index_map)


def _mm_res(a, w, res, tm, tn, name):
    m, k = a.shape
    n = w.shape[1]
    return pl.pallas_call(
        _mm_res_kernel,
        grid=(m // tm, n // tn),
        in_specs=[
            pl.BlockSpec((tm, k), lambda i, j: (i, 0)),
            _resident((k, tn), lambda i, j: (0, j), n // tn),
            pl.BlockSpec((tm, tn), lambda i, j: (i, j)),
        ],
        out_specs=pl.BlockSpec((tm, tn), lambda i, j: (i, j)),
        out_shape=jax.ShapeDtypeStruct((m, n), F32),
        compiler_params=_params(("parallel", "arbitrary")),
        name=name,
    )(a, w, res)


def _ffn_up_kernel(nj, blocks_per_seq, x_ref, gain_ref, wg_ref, wv_ref, cg_ref, cv_ref, bg_ref, bv_ref, out_ref,
                   h_ref, uga_ref, uva_ref, ugb_ref, uvb_ref, tg_ref, tv_ref):
    t = pl.program_id(0)
    n_tiles = pl.num_programs(0) - 1
    tc = jnp.minimum(t, n_tiles - 1)
    i = tc // nj
    j = tc - i * nj
    tm, tn = out_ref.shape
    n_slabs = tn // LANES
    sub = 128

    @pl.when(t == 0)
    def _():
        ugb_ref[...] = jnp.zeros_like(ugb_ref)
        uvb_ref[...] = jnp.zeros_like(uvb_ref)

    @pl.when(j == 0)
    def _():
        _rms_rows_to(x_ref, gain_ref, h_ref)

    first = lax.rem(i, blocks_per_seq) == 0

    def step(ug_new, uv_new, ug_old, uv_old):
        def conv(u_ref, c_ref, b_ref, s, r):
            cols = slice(s * LANES, (s + 1) * LANES)
            acc = b_ref[:, cols] + c_ref[CONV_FFN - 1:CONV_FFN, cols] * u_ref[s, HALO + r * sub:HALO + (r + 1) * sub, :]
            for k in range(CONV_FFN - 1):
                off = HALO - (CONV_FFN - 1) + k + r * sub
                acc = acc + c_ref[k:k + 1, cols] * u_ref[s, pl.ds(off, sub, stride=1), :]
            return acc

        for s in range(n_slabs):
            for r in range(tm // sub):
                gate = conv(ug_old, cg_ref, bg_ref, s, r)
                val = conv(uv_old, cv_ref, bv_ref, s, r)
                out_ref[r * sub:(r + 1) * sub, s * LANES:(s + 1) * LANES] = (jax.nn.silu(gate) * val).astype(out_ref.dtype)

        h = h_ref[...]
        for w_ref, u_ref, t_ref in ((wg_ref, ug_new, tg_ref), (wv_ref, uv_new, tv_ref)):
            res = jnp.dot(h, w_ref[...], preferred_element_type=F32)
            for s in range(n_slabs):
                u_ref[s, 0:HALO, :] = jnp.where(first, 0.0, t_ref[j, s])
                u_ref[s, HALO:, :] = res[:, s * LANES:(s + 1) * LANES]
                t_ref[j, s] = u_ref[s, tm:tm + HALO, :]

    even = lax.rem(t, 2) == 0

    @pl.when(even)
    def _():
        step(uga_ref, uva_ref, ugb_ref, uvb_ref)

    @pl.when(jnp.logical_not(even))
    def _():
        step(ugb_ref, uvb_ref, uga_ref, uva_ref)


def _ffn_up(xf, gain, w_up, conv_w, conv_b, seq, tm, tn):
    m, d = xf.shape
    nj = D_FF // tn
    n_tiles = (m // tm) * nj
    b2 = conv_b.reshape(1, 2 * D_FF).astype(F32)
    now = lambda t: jnp.minimum(t, n_tiles - 1)
    lag = lambda t: jnp.maximum(t - 1, 0)
    return pl.pallas_call(
        functools.partial(_ffn_up_kernel, nj, seq // tm),
        grid=(n_tiles + 1,),
        in_specs=[
            pl.BlockSpec((tm, d), lambda t: (now(t) // nj, 0)),
            pl.BlockSpec((1, d), lambda t: (0, 0)),
            pl.BlockSpec((d, tn), lambda t: (0, now(t) % nj)),
            pl.BlockSpec((d, tn), lambda t: (0, nj + now(t) % nj)),
            pl.BlockSpec((CONV_FFN, tn), lambda t: (0, lag(t) % nj)),
            pl.BlockSpec((CONV_FFN, tn), lambda t: (0, nj + lag(t) % nj)),
            pl.BlockSpec((1, tn), lambda t: (0, lag(t) % nj)),
            pl.BlockSpec((1, tn), lambda t: (0, nj + lag(t) % nj)),
        ],
        out_specs=pl.BlockSpec((tm, tn), lambda t: (lag(t) // nj, lag(t) % nj)),
        out_shape=jax.ShapeDtypeStruct((m, D_FF), BF16),
        scratch_shapes=[
            pltpu.VMEM((tm, d), BF16),
            pltpu.VMEM((tn // LANES, tm + HALO, LANES), F32),
            pltpu.VMEM((tn // LANES, tm + HALO, LANES), F32),
            pltpu.VMEM((tn // LANES, tm + HALO, LANES), F32),
            pltpu.VMEM((tn // LANES, tm + HALO, LANES), F32),
            pltpu.VMEM((nj, tn // LANES, HALO, LANES), F32),
            pltpu.VMEM((nj, tn // LANES, HALO, LANES), F32),
        ],
        compiler_params=_params(("arbitrary",)),
        name="ffn_up",
    )(xf, gain.reshape(1, d), w_up, w_up, conv_w, conv_w, b2, b2)


def _ple_kernel(last, x_ref, gain_ref, wg_ref, p_ref, wp_ref, fgain_ref, out_ref, h_ref):
    tm = out_ref.shape[0]
    hb = tm // 2
    _rms_rows_to(x_ref, gain_ref, h_ref)
    for r in range(2):
        rows = slice(r * hb, (r + 1) * hb)
        gate = jax.nn.sigmoid(jnp.dot(h_ref[rows, :], wg_ref[...], preferred_element_type=F32))
        emb = jnp.dot(p_ref[rows, :].astype(BF16), wp_ref[...], preferred_element_type=F32)
        y = x_ref[rows, :] + gate * emb
        if last:
            y = (y * lax.rsqrt(jnp.mean(y * y, axis=-1, keepdims=True) + EPS)) * fgain_ref[...]
        out_ref[rows, :] = y


def _ple(xf, gain, wg, p3, layer, wp, final_gain, last, tm):
    m, d = xf.shape
    pd = p3.shape[2]
    return pl.pallas_call(
        functools.partial(_ple_kernel, last),
        grid=(m // tm,),
        in_specs=[
            pl.BlockSpec((tm, d), lambda i: (i, 0)),
            pl.BlockSpec((1, d), lambda i: (0, 0)),
            _resident((d, d), lambda i: (0, 0), 1),
            pl.BlockSpec((None, tm, pd), lambda i: (layer, i, 0)),
            _resident((pd, d), lambda i: (0, 0), 1),
            pl.BlockSpec((1, d), lambda i: (0, 0)),
        ],
        out_specs=pl.BlockSpec((tm, d), lambda i: (i, 0)),
        out_shape=jax.ShapeDtypeStruct((m, d), F32),
        scratch_shapes=[pltpu.VMEM((tm, d), BF16)],
        compiler_params=_params(("parallel",)),
        name="ple",
    )(xf, gain.reshape(1, d), wg, p3, wp, final_gain.reshape(1, d))


def kernel(x, p, norm_mix, w_in, conv_qkv, a_log, dt_bias, head_norm, sgu_norm, w_spatial, b_spatial, w_branch_a, w_branch_b, w_out, norm_ffn, w_ffn_up, conv_ffn, b_conv_ffn, w_ffn_down, norm_ple, w_ple_gate, w_ple_proj, norm_final):
    batch, seq, d = x.shape
    depth = w_in.shape[0]
    m = batch * seq
    assert d == V_WIDTH and seq % CHUNK_B == 0 and w_in.shape[2] == N_MAIN + N_NARROW
    tm = _row_tile(seq, 1024)
    tm_res = _row_tile(seq, 512)
    xf = x.reshape(m, d).astype(F32)
    p3 = p.reshape(depth, m, p.shape[-1])
    for i in range(depth):
        w_main, w_nar = _repack_w_in(w_in.astype(F32), i)
        proj, nar = _inproj(xf, norm_mix[i], w_main, w_nar, tm, _col_tile(N_MAIN, 1024))
        o = _delta(proj, nar, conv_qkv[i].astype(F32), a_log[i], dt_bias[i], head_norm[i].astype(F32), batch, seq)
        s = _sgu(proj, sgu_norm[i].astype(F32), w_spatial[i].astype(F32), b_spatial[i], _row_tile(seq, 512))
        merged = _merge(o, s, proj, w_branch_a[i].astype(BF16), w_branch_b[i].astype(BF16), tm, 512)
        xf = _mm_res(merged, w_out[i].astype(BF16), xf, tm_res, d, "out_proj")
        act = _ffn_up(xf, norm_ffn[i].astype(F32), w_ffn_up[i].astype(BF16), conv_ffn[i].astype(F32),
                      b_conv_ffn[i], seq, tm, _col_tile(D_FF, 512))
        xf = _mm_res(act, w_ffn_down[i].astype(BF16), xf, tm, 512, "ffn_down")
        xf = _ple(xf, norm_ple[i].astype(F32), w_ple_gate[i].astype(BF16), p3, i, w_ple_proj[i].astype(BF16),
                  norm_final.astype(F32), i == depth - 1, tm_res)
    return xf.reshape(batch, seq, d).astype(x.dtype)
```

```python
import functools

import jax
import jax.numpy as jnp
from jax import lax
from jax.experimental import pallas as pl
from jax.experimental.pallas import tpu as pltpu

F32 = jnp.float32
BF16 = jnp.bfloat16
EPS = 1e-6

HEAD_DIM = 128
N_QK_HEADS = 8
N_V_HEADS = 16
QK_WIDTH = N_QK_HEADS * HEAD_DIM
V_WIDTH = N_V_HEADS * HEAD_DIM
CONV_A = 4
CHUNK_A = 64
N_GROUPS_B = 8
GROUP_DIM_B = 128
WIDTH_B = N_GROUPS_B * GROUP_DIM_B
CHUNK_B = 128
D_FF = 5632
CONV_FFN = 3

COL_Q = 0
COL_K = QK_WIDTH
COL_V = 2 * QK_WIDTH
COL_Z = COL_V + V_WIDTH
COL_NARROW = COL_Z + V_WIDTH
N_NARROW = 2 * N_V_HEADS
COL_U = COL_Z + V_WIDTH
COL_VB = COL_U + WIDTH_B
COL_GA = COL_VB + WIDTH_B
N_MAIN = COL_GA + 2 * 2048
LANES = 128
HALO = 8

V7X_VMEM_LIMIT_BYTES = 52 * 1024 * 1024


def _params(sem):
    return pltpu.CompilerParams(dimension_semantics=sem, vmem_limit_bytes=V7X_VMEM_LIMIT_BYTES)


def _row_tile(s, target):
    t = min(target, s)
    while s % t:
        t //= 2
    return t


def _col_tile(n, target):
    t = min(target, n)
    while n % t or t % LANES:
        t -= LANES
    return t


def _rms_rows_to(x_ref, gain_ref, h_ref, rows=32):
    tm = x_ref.shape[0]

    def body(r, carry):
        sl = pl.ds(pl.multiple_of(r * rows, rows), rows)
        x = x_ref[sl, :]
        ms = jnp.mean(x * x, axis=-1, keepdims=True)
        h_ref[sl, :] = ((x * lax.rsqrt(ms + EPS)) * gain_ref[...]).astype(h_ref.dtype)
        return carry

    lax.fori_loop(0, tm // rows, body, 0, unroll=2)


_NT = (((1,), (1,)), ((), ()))


def _inproj_kernel(x_ref, gain_ref, w_ref, wn_ref, out_ref, nar_ref, h_ref):
    @pl.when((pl.program_id(1) == 0) & (pl.program_id(2) == 0))
    def _():
        _rms_rows_to(x_ref, gain_ref, h_ref)
        nar_ref[...] = lax.dot_general(h_ref[...], wn_ref[...], _NT, preferred_element_type=F32)

    out_ref[...] = lax.dot_general(h_ref[...], w_ref[...], _NT, preferred_element_type=F32).astype(out_ref.dtype)


def _inproj(xf, gain, wt_all, layer, tm, tn):
    m, d = xf.shape
    n_in = wt_all.shape[1]
    nj = COL_NARROW // tn
    assert COL_NARROW % tn == 0 and N_MAIN == 2 * COL_NARROW
    wt2 = wt_all.reshape(-1, d)
    base = layer * n_in
    q = N_NARROW
    assert n_in % q == 0 and COL_NARROW % q == 0 and tn % q == 0
    return pl.pallas_call(
        _inproj_kernel,
        grid=(m // tm, 2, nj),
        in_specs=[
            pl.BlockSpec((tm, d), lambda i, g, j: (i, 0)),
            pl.BlockSpec((1, d), lambda i, g, j: (0, 0)),
            pl.BlockSpec((pl.Element(tn), pl.Element(d)),
                         lambda i, g, j: ((base // q + g * ((COL_NARROW + N_NARROW) // q) + j * (tn // q)) * q, 0)),
            pl.BlockSpec((pl.Element(LANES), pl.Element(d)), lambda i, g, j: (base + COL_NARROW, 0)),
        ],
        out_specs=[
            pl.BlockSpec((tm, tn), lambda i, g, j: (i, g * nj + j)),
            pl.BlockSpec((tm, LANES), lambda i, g, j: (i, 0)),
        ],
        out_shape=[jax.ShapeDtypeStruct((m, N_MAIN), BF16), jax.ShapeDtypeStruct((m, LANES), F32)],
        scratch_shapes=[pltpu.VMEM((tm, d), BF16)],
        compiler_params=_params(("parallel", "arbitrary", "arbitrary")),
        name="inproj",
    )(xf, gain.reshape(1, d), wt2, wt2)


def _softplus(x):
    return jnp.maximum(x, 0.0) + jnp.log(1.0 + jnp.exp(-jnp.abs(x)))


def _delta_kernel(nh, group, q_ref, k_ref, v_ref, z_ref, nar_ref, cq_ref, ck_ref, cv_ref, alog_ref, dtb_ref, hg_ref,
                  o_ref, s_ref, cs_ref, lhs_ref, kt_ref, bd_ref, dc_ref, x0_ref, wq_ref, u_ref, at_ref, kd_ref, cd_ref):
    c = CHUNK_A
    hd = HEAD_DIM
    n_groups = q_ref.shape[0] // (group * c)
    hq0 = pl.program_id(1) * nh
    for ref in (s_ref, cs_ref, lhs_ref, kt_ref, bd_ref, dc_ref, x0_ref, wq_ref, u_ref, at_ref, kd_ref, cd_ref):
        ref[...] = jnp.zeros_like(ref)

    row = lax.broadcasted_iota(jnp.int32, (2 * c, 2 * c), 0)
    col = lax.broadcasted_iota(jnp.int32, (2 * c, 2 * c), 1)
    same_head = (row < c) == (col < c)
    causal = same_head & (row >= col)
    strict = same_head & (row > col)
    lane_lo = col < c
    lane_m = col - jnp.where(row < c, 0, 1)
    row_m = row - jnp.where(lane_lo, 0, 1)
    row_c = lax.broadcasted_iota(jnp.int32, (c, LANES), 0)
    neg_exp_alog = -jnp.exp(alog_ref[...])
    q_scale = HEAD_DIM ** -0.5
    heads = [(hh, h) for hh in range(nh) for h in range(2)]

    def dots(lhs_list, rhs_list):
        return [jnp.dot(a, b, preferred_element_type=F32) for a, b in zip(lhs_list, rhs_list)]

    def gates(ci):
        rows = pl.ds(pl.multiple_of(ci * c, c), c)
        nar = nar_ref[rows, :]
        sig = jax.nn.sigmoid(nar)
        gam = neg_exp_alog * _softplus(nar + dtb_ref[...])
        for sh in (1, 2, 4, 8, 16, 32):
            gam = gam + jnp.where(row_c >= sh, pltpu.roll(gam, sh, axis=0), 0.0)
        sig2 = jnp.concatenate([sig, sig], axis=0)
        gam2 = jnp.concatenate([gam, gam], axis=0)
        return sig2, gam2, gam2.T

    def conv_group(c0, hh):
        n = group * c
        rows = pl.ds(pl.multiple_of(c0 * c, c), n)
        srcs = ((q_ref, cq_ref, hh * hd), (k_ref, ck_ref, hh * hd),
                (v_ref, cv_ref, hh * 2 * hd), (v_ref, cv_ref, hh * 2 * hd + hd))
        outs = []
        for s, (x_ref, w_ref, c_lo) in enumerate(srcs):
            cols = slice(c_lo, c_lo + hd)
            cs_ref[hh, s, HALO:HALO + n, :] = x_ref[rows, cols].astype(F32)
            per_chunk = []
            for g in range(group):
                acc = w_ref[CONV_A - 1:CONV_A, cols] * cs_ref[hh, s, HALO + g * c:HALO + (g + 1) * c, :]
                for j in range(CONV_A - 1):
                    off = HALO - (CONV_A - 1) + j + g * c
                    acc = acc + w_ref[j:j + 1, cols] * cs_ref[hh, s, pl.ds(off, c, stride=1), :]
                per_chunk.append(jax.nn.silu(acc))
            outs.append(per_chunk)
            cs_ref[hh, s, 0:HALO, :] = cs_ref[hh, s, n:n + HALO, :]
        return outs

    def prep(grp, par, ring):
        c0 = grp * group
        conv = [conv_group(c0, hh) for hh in range(nh)]
        for g in range(group):
            sig2, gam2, gam2t = gates(c0 + g)
            for hh in range(nh):
                hq2 = 2 * (hq0 + hh)
                qc, kc, v_lo, v_hi = (conv[hh][s][g] for s in range(4))
                qn = qc * (lax.rsqrt(jnp.sum(qc * qc, axis=-1, keepdims=True) + EPS) * q_scale)
                kn = kc * lax.rsqrt(jnp.sum(kc * kc, axis=-1, keepdims=True) + EPS)
                beta_c = jnp.sum(jnp.where(lane_m == hq2, sig2, 0.0), axis=1, keepdims=True)
                gam_c = jnp.sum(jnp.where(lane_m == hq2 + N_V_HEADS, gam2, 0.0), axis=1, keepdims=True)
                gam_r = jnp.sum(jnp.where(row_m == hq2 + N_V_HEADS, gam2t, 0.0), axis=0, keepdims=True)
                gam_cb = jnp.broadcast_to(gam_c, (2 * c, 2 * c))
                decay = jnp.where(causal, jnp.exp(jnp.where(causal, gam_cb - gam_r, 0.0)), 0.0)
                egam_c = jnp.exp(gam_c)
                glast_r = jnp.where(lane_lo[0:1, :], gam_cb[c - 1:c, :], gam_cb[2 * c - 1:2 * c, :])
                kn2 = jnp.concatenate([kn, kn], axis=0)
                kt2 = kn2.T
                kdec = kt2 * jnp.exp(glast_r - gam_r)
                qd2 = jnp.concatenate([qn, qn], axis=0) * egam_c
                slot = ring * group + g
                for h in range(2):
                    keep = lane_lo if h == 0 else jnp.logical_not(lane_lo)
                    kd_ref[slot, hh, h] = jnp.where(keep, kdec, 0.0).astype(BF16)
                    cd = jnp.exp(gam_cb[(h + 1) * c - 1:(h + 1) * c, :])
                    cd_ref[slot, hh, h] = jnp.broadcast_to(cd, (HALO, LANES))
                    wq_ref[slot, hh, h, c:2 * c, :] = qd2[h * c:(h + 1) * c].astype(BF16)
                v2 = jnp.concatenate([v_lo, v_hi], axis=0)
                lhs_ref[par, g, hh] = jnp.concatenate([qn, kn], axis=0).astype(BF16)
                kt_ref[par, g, hh] = kt2.astype(BF16)
                dc_ref[par, g, hh] = decay
                bd_ref[par, g, hh] = jnp.where(strict, beta_c * decay, 0.0)
                x0_ref[par, g, hh] = jnp.concatenate([v2 * beta_c, kn2 * (beta_c * egam_c)], axis=1)

    def solve(par, ring):
        items = [(g, hh) for g in range(group) for hh in range(nh)]
        rs = dots([lhs_ref[par, g, hh] for g, hh in items], [kt_ref[par, g, hh] for g, hh in items])
        yield
        pws, xs = [], []
        for (g, hh), r in zip(items, rs):
            qk2 = jnp.concatenate([r[:c], r[:c]], axis=0)
            kk2 = jnp.concatenate([r[c:], r[c:]], axis=0)
            at_ref[ring * group + g, hh] = (qk2 * dc_ref[par, g, hh]).astype(BF16)
            pws.append((-(kk2 * bd_ref[par, g, hh])).astype(BF16))
            xs.append(x0_ref[par, g, hh])
        for level in range(6):
            if level:
                pws = [p.astype(BF16) for p in dots(pws, pws)]
            xs = [x + d for x, d in zip(xs, dots(pws, [x.astype(BF16) for x in xs]))]
            yield
        for (g, hh), x in zip(items, xs):
            slot = ring * group + g
            u_ref[slot, hh] = x[:, :hd]
            for h in range(2):
                wq_ref[slot, hh, h, 0:c, :] = x[h * c:(h + 1) * c, hd:].astype(BF16)

    def scan(grp, ring):
        for g in range(group):
            slot = ring * group + g
            rows = pl.ds(pl.multiple_of((grp * group + g) * c, c), c)
            res = dots([wq_ref[slot, hh, h] for hh, h in heads], [s_ref[hh, h].astype(BF16) for hh, h in heads])
            yield
            vnew_b, o_state = [], []
            for hh in range(nh):
                u = u_ref[slot, hh]
                r0, r1 = res[2 * hh], res[2 * hh + 1]
                vnew_b.append(jnp.concatenate([u[:c] - r0[:c], u[c:] - r1[:c]], axis=0).astype(BF16))
                o_state.append(jnp.concatenate([r0[c:], r1[c:]], axis=0))
            upd = dots([kd_ref[slot, hh, h] for hh, h in heads], [vnew_b[hh] for hh, h in heads])
            o_intra = dots([at_ref[slot, hh] for hh in range(nh)], vnew_b)
            yield
            for (hh, h), d in zip(heads, upd):
                s_ref[hh, h] = s_ref[hh, h] * cd_ref[slot, hh, h][0:1, :] + d
            for hh in range(nh):
                o2 = o_state[hh] + o_intra[hh]
                zc = z_ref[rows, hh * 2 * hd:(hh + 1) * 2 * hd].astype(F32)
                z2 = jnp.concatenate([zc[:, :hd], zc[:, hd:]], axis=0)
                y = (o2 * lax.rsqrt(jnp.mean(o2 * o2, axis=-1, keepdims=True) + EPS)) * hg_ref[...]
                y = y * jax.nn.silu(z2)
                o_ref[rows, hh * 2 * hd:hh * 2 * hd + hd] = y[:c].astype(o_ref.dtype)
                o_ref[rows, hh * 2 * hd + hd:(hh + 1) * 2 * hd] = y[c:].astype(o_ref.dtype)

    def body(t, carry):
        last = n_groups - 1
        scan_it = scan(jnp.clip(t - 2, 0, last), lax.rem(t + 1, 3))
        solve_it = solve(lax.rem(t + 1, 2), lax.rem(t + 2, 3))
        for turn in range(2 * group + 7):
            if turn % 2 == 0 or turn >= 4 * group:
                next(solve_it, None)
            else:
                next(scan_it, None)
        for _ in scan_it:
            pass
        for _ in solve_it:
            pass
        prep(jnp.minimum(t, last), lax.rem(t, 2), lax.rem(t, 3))
        return carry

    lax.fori_loop(0, n_groups + 2, body, 0)


def _delta(proj, nar, conv_w, a_log, dt_bias, head_gain, batch, seq, nh=2, group=2):
    m = proj.shape[0]
    hd = HEAD_DIM
    c2 = 2 * CHUNK_A
    assert N_QK_HEADS % nh == 0 and seq % (group * CHUNK_A) == 0
    pad = lambda v: jnp.zeros((1, LANES), F32).at[0, N_V_HEADS:2 * N_V_HEADS].set(v.astype(F32))
    qw, vw = nh * hd, nh * 2 * hd
    qb, kb, vb, zb = COL_Q // qw, COL_K // qw, COL_V // vw, COL_Z // vw
    ring = 3 * group
    return pl.pallas_call(
        functools.partial(_delta_kernel, nh, group),
        grid=(batch, N_QK_HEADS // nh),
        in_specs=[
            pl.BlockSpec((seq, qw), lambda b, h: (b, qb + h)),
            pl.BlockSpec((seq, qw), lambda b, h: (b, kb + h)),
            pl.BlockSpec((seq, vw), lambda b, h: (b, vb + h)),
            pl.BlockSpec((seq, vw), lambda b, h: (b, zb + h)),
            pl.BlockSpec((seq, LANES), lambda b, h: (b, 0)),
            pl.BlockSpec((CONV_A, qw), lambda b, h: (0, qb + h)),
            pl.BlockSpec((CONV_A, qw), lambda b, h: (0, kb + h)),
            pl.BlockSpec((CONV_A, vw), lambda b, h: (0, vb + h)),
            pl.BlockSpec((1, LANES), lambda b, h: (0, 0)),
            pl.BlockSpec((1, LANES), lambda b, h: (0, 0)),
            pl.BlockSpec((1, hd), lambda b, h: (0, 0)),
        ],
        out_specs=pl.BlockSpec((seq, vw), lambda b, h: (b, h)),
        out_shape=jax.ShapeDtypeStruct((m, V_WIDTH), BF16),
        scratch_shapes=[
            pltpu.VMEM((nh, 2, hd, hd), F32),
            pltpu.VMEM((nh, 4, 2 * HALO + group * CHUNK_A, LANES), F32),
            pltpu.VMEM((2, group, nh, c2, hd), BF16),
            pltpu.VMEM((2, group, nh, hd, c2), BF16),
            pltpu.VMEM((2, group, nh, c2, c2), F32),
            pltpu.VMEM((2, group, nh, c2, c2), F32),
            pltpu.VMEM((2, group, nh, c2, 2 * hd), F32),
            pltpu.VMEM((ring, nh, 2, c2, hd), BF16),
            pltpu.VMEM((ring, nh, c2, hd), F32),
            pltpu.VMEM((ring, nh, c2, c2), BF16),
            pltpu.VMEM((ring, nh, 2, hd, c2), BF16),
            pltpu.VMEM((ring, nh, 2, HALO, LANES), F32),
        ],
        compiler_params=_params(("parallel", "parallel")),
        name="delta",
    )(proj, proj, proj, proj, nar, conv_w, conv_w, conv_w, pad(a_log), pad(dt_bias), head_gain.reshape(1, hd))


def _sgu_kernel(u_ref, v_ref, gain_ref, w_ref, b_ref, o_ref):
    t = CHUNK_B
    g_dim = GROUP_DIM_B
    row = lax.broadcasted_iota(jnp.int32, (t, t), 0)
    col = lax.broadcasted_iota(jnp.int32, (t, t), 1)
    keep = row >= col

    def chunk(ci, carry):
        rows = pl.ds(pl.multiple_of(ci * t, t), t)
        v = jax.nn.gelu(v_ref[rows, :].astype(F32))
        ms = jnp.mean(v * v, axis=-1, keepdims=True)
        vn = ((v * lax.rsqrt(ms + EPS)) * gain_ref[...]).astype(BF16)
        for g in range(N_GROUPS_B):
            cols = slice(g * g_dim, (g + 1) * g_dim)
            wm = jnp.where(keep, w_ref[g], 0.0).astype(BF16)
            mixed = jnp.dot(wm, vn[:, cols], preferred_element_type=F32) + b_ref[g]
            u = jax.nn.gelu(u_ref[rows, cols].astype(F32))
            o_ref[rows, cols] = (u * mixed).astype(o_ref.dtype)
        return carry

    lax.fori_loop(0, u_ref.shape[0] // t, chunk, 0)


def _sgu(proj, gain, w_s, b_s, tr):
    m = proj.shape[0]
    wb = WIDTH_B
    b_bc = jnp.broadcast_to(b_s.astype(F32)[:, :, None], (N_GROUPS_B, CHUNK_B, GROUP_DIM_B))
    return pl.pallas_call(
        _sgu_kernel,
        grid=(m // tr,),
        in_specs=[
            pl.BlockSpec((tr, wb), lambda i: (i, COL_U // wb)),
            pl.BlockSpec((tr, wb), lambda i: (i, COL_VB // wb)),
            pl.BlockSpec((1, wb), lambda i: (0, 0)),
            pl.BlockSpec((N_GROUPS_B, CHUNK_B, CHUNK_B), lambda i: (0, 0, 0)),
            pl.BlockSpec((N_GROUPS_B, CHUNK_B, GROUP_DIM_B), lambda i: (0, 0, 0)),
        ],
        out_specs=pl.BlockSpec((tr, wb), lambda i: (i, 0)),
        out_shape=jax.ShapeDtypeStruct((m, wb), BF16),
        compiler_params=_params(("parallel",)),
        name="sgu",
    )(proj, proj, gain.reshape(1, wb), w_s, b_bc)


def _merge_kernel(o_ref, s_ref, ga_ref, gb_ref, wa_ref, wb_ref, out_ref):
    tm = out_ref.shape[0]
    hb = tm // 2
    for r in range(2):
        rows = slice(r * hb, (r + 1) * hb)
        ya = jnp.dot(o_ref[rows, :], wa_ref[...], preferred_element_type=F32)
        yb = jnp.dot(s_ref[rows, :], wb_ref[...], preferred_element_type=F32)
        ga = jax.nn.sigmoid(ga_ref[rows, :].astype(F32))
        gb = jax.nn.sigmoid(gb_ref[rows, :].astype(F32))
        out_ref[rows, :] = (ga * ya + gb * yb).astype(out_ref.dtype)


def _merge(o, s, proj, wa, wb, layer, tm, tn):
    m = o.shape[0]
    d = wa.shape[2]
    ga0 = COL_GA // tn
    gb0 = (COL_GA + d) // tn
    return pl.pallas_call(
        _merge_kernel,
        grid=(m // tm, d // tn),
        in_specs=[
            pl.BlockSpec((tm, o.shape[1]), lambda i, j: (i, 0)),
            pl.BlockSpec((tm, s.shape[1]), lambda i, j: (i, 0)),
            pl.BlockSpec((tm, tn), lambda i, j: (i, ga0 + j)),
            pl.BlockSpec((tm, tn), lambda i, j: (i, gb0 + j)),
            pl.BlockSpec((None, wa.shape[1], tn), lambda i, j: (layer, 0, j)),
            pl.BlockSpec((None, wb.shape[1], tn), lambda i, j: (layer, 0, j)),
        ],
        out_specs=pl.BlockSpec((tm, tn), lambda i, j: (i, j)),
        out_shape=jax.ShapeDtypeStruct((m, d), BF16),
        compiler_params=_params(("parallel", "arbitrary")),
        name="merge",
    )(o, s, proj, proj, wa, wb)


def _mm_res_kernel(a_ref, w_ref, res_ref, out_ref):
    tm = out_ref.shape[0]
    hb = tm // 2
    for r in range(2):
        rows = slice(r * hb, (r + 1) * hb)
        out_ref[rows, :] = res_ref[rows, :] + jnp.dot(a_ref[rows, :], w_ref[...], preferred_element_type=F32)


def _resident(block_shape, index_map, n_blocks):
    if n_blocks == 1:
        return pl.BlockSpec(block_shape, index_map, pipeline_mode=pl.Buffered(1))
    return pl.BlockSpec(block_shape, index_map)


def _mm_res(a, w, layer, res, tm, tn, name):
    m, k = a.shape
    n = w.shape[2]
    return pl.pallas_call(
        _mm_res_kernel,
        grid=(m // tm, n // tn),
        in_specs=[
            pl.BlockSpec((tm, k), lambda i, j: (i, 0)),
            _resident((None, k, tn), lambda i, j: (layer, 0, j), n // tn),
            pl.BlockSpec((tm, tn), lambda i, j: (i, j)),
        ],
        out_specs=pl.BlockSpec((tm, tn), lambda i, j: (i, j)),
        out_shape=jax.ShapeDtypeStruct((m, n), F32),
        compiler_params=_params(("parallel", "arbitrary")),
        name=name,
    )(a, w, res)


def _ffn_up_kernel(blocks_per_seq, x_ref, gain_ref, wg_ref, wv_ref, cg_ref, cv_ref, bg_ref, bv_ref, out_ref,
                   h_ref, ug_ref, uv_ref, tg_ref, tv_ref):
    i = pl.program_id(0)
    j = pl.program_id(1)
    tm, tn = out_ref.shape
    half = min(tn, 2 * LANES)
    sub = 128

    @pl.when(j == 0)
    def _():
        _rms_rows_to(x_ref, gain_ref, h_ref)

    first = (i % blocks_per_seq) == 0
    h = h_ref[...]
    for c0 in range(0, tn, half):
        slabs = range(c0 // LANES, (c0 + half) // LANES)
        for w_ref, u_ref, t_ref in ((wg_ref, ug_ref, tg_ref), (wv_ref, uv_ref, tv_ref)):
            res = jnp.dot(h, w_ref[:, c0:c0 + half], preferred_element_type=F32)
            for s in slabs:
                u_ref[s, 0:HALO, :] = jnp.where(first, 0.0, t_ref[j, s])
                u_ref[s, HALO:, :] = res[:, s * LANES - c0:(s + 1) * LANES - c0]
                t_ref[j, s] = u_ref[s, tm:tm + HALO, :]

        def conv(u_ref, c_ref, b_ref, s, r):
            cols = slice(s * LANES, (s + 1) * LANES)
            acc = b_ref[:, cols] + c_ref[CONV_FFN - 1:CONV_FFN, cols] * u_ref[s, HALO + r * sub:HALO + (r + 1) * sub, :]
            for t in range(CONV_FFN - 1):
                off = HALO - (CONV_FFN - 1) + t + r * sub
                acc = acc + c_ref[t:t + 1, cols] * u_ref[s, pl.ds(off, sub, stride=1), :]
            return acc

        for s in slabs:
            for r in range(tm // sub):
                gate = conv(ug_ref, cg_ref, bg_ref, s, r)
                val = conv(uv_ref, cv_ref, bv_ref, s, r)
                out_ref[r * sub:(r + 1) * sub, s * LANES:(s + 1) * LANES] = (jax.nn.silu(gate) * val).astype(out_ref.dtype)


def _ffn_up(xf, gain, w_up, layer, conv_w, conv_b, seq, tm, tn):
    m, d = xf.shape
    nj = D_FF // tn
    b2 = conv_b.reshape(1, 2 * D_FF).astype(F32)
    return pl.pallas_call(
        functools.partial(_ffn_up_kernel, seq // tm),
        grid=(m // tm, nj),
        in_specs=[
            pl.BlockSpec((tm, d), lambda i, j: (i, 0)),
            pl.BlockSpec((1, d), lambda i, j: (0, 0)),
            pl.BlockSpec((None, d, tn), lambda i, j: (layer, 0, j)),
            pl.BlockSpec((None, d, tn), lambda i, j: (layer, 0, nj + j)),
            pl.BlockSpec((CONV_FFN, tn), lambda i, j: (0, j)),
            pl.BlockSpec((CONV_FFN, tn), lambda i, j: (0, nj + j)),
            pl.BlockSpec((1, tn), lambda i, j: (0, j)),
            pl.BlockSpec((1, tn), lambda i, j: (0, nj + j)),
        ],
        out_specs=pl.BlockSpec((tm, tn), lambda i, j: (i, j)),
        out_shape=jax.ShapeDtypeStruct((m, D_FF), BF16),
        scratch_shapes=[
            pltpu.VMEM((tm, d), BF16),
            pltpu.VMEM((tn // LANES, tm + HALO, LANES), F32),
            pltpu.VMEM((tn // LANES, tm + HALO, LANES), F32),
            pltpu.VMEM((nj, tn // LANES, HALO, LANES), F32),
            pltpu.VMEM((nj, tn // LANES, HALO, LANES), F32),
        ],
        compiler_params=_params(("arbitrary", "arbitrary")),
        name="ffn_up",
    )(xf, gain.reshape(1, d), w_up, w_up, conv_w, conv_w, b2, b2)


def _ple_kernel(last, x_ref, gain_ref, wg_ref, p_ref, wp_ref, fgain_ref, out_ref, h_ref):
    tm = out_ref.shape[0]
    hb = tm // 2
    _rms_rows_to(x_ref, gain_ref, h_ref)
    for r in range(2):
        rows = slice(r * hb, (r + 1) * hb)
        gate = jax.nn.sigmoid(jnp.dot(h_ref[rows, :], wg_ref[...], preferred_element_type=F32))
        emb = jnp.dot(p_ref[rows, :].astype(BF16), wp_ref[...], preferred_element_type=F32)
        y = x_ref[rows, :] + gate * emb
        if last:
            y = (y * lax.rsqrt(jnp.mean(y * y, axis=-1, keepdims=True) + EPS)) * fgain_ref[...]
        out_ref[rows, :] = y


def _ple(xf, gain, wg, p3, layer, wp, final_gain, last, tm):
    m, d = xf.shape
    pd = p3.shape[2]
    return pl.pallas_call(
        functools.partial(_ple_kernel, last),
        grid=(m // tm,),
        in_specs=[
            pl.BlockSpec((tm, d), lambda i: (i, 0)),
            pl.BlockSpec((1, d), lambda i: (0, 0)),
            _resident((None, d, d), lambda i: (layer, 0, 0), 1),
            pl.BlockSpec((None, tm, pd), lambda i: (layer, i, 0)),
            _resident((None, pd, d), lambda i: (layer, 0, 0), 1),
            pl.BlockSpec((1, d), lambda i: (0, 0)),
        ],
        out_specs=pl.BlockSpec((tm, d), lambda i: (i, 0)),
        out_shape=jax.ShapeDtypeStruct((m, d), F32),
        scratch_shapes=[pltpu.VMEM((tm, d), BF16)],
        compiler_params=_params(("parallel",)),
        name="ple",
    )(xf, gain.reshape(1, d), wg, p3, wp, final_gain.reshape(1, d))


def kernel(x, p, norm_mix, w_in, conv_qkv, a_log, dt_bias, head_norm, sgu_norm, w_spatial, b_spatial, w_branch_a, w_branch_b, w_out, norm_ffn, w_ffn_up, conv_ffn, b_conv_ffn, w_ffn_down, norm_ple, w_ple_gate, w_ple_proj, norm_final):
    batch, seq, d = x.shape
    depth = w_in.shape[0]
    m = batch * seq
    assert d == V_WIDTH and seq % CHUNK_B == 0 and w_in.shape[2] == N_MAIN + N_NARROW
    tm = _row_tile(seq, 1024)
    tm_res = _row_tile(seq, 512)
    xf = x.reshape(m, d).astype(F32)
    p3 = p.reshape(depth, m, p.shape[-1])
    wa_b, wb_b, wo_b = w_branch_a.astype(BF16), w_branch_b.astype(BF16), w_out.astype(BF16)
    wu_b, wd_b = w_ffn_up.astype(BF16), w_ffn_down.astype(BF16)
    wg_b, wp_b = w_ple_gate.astype(BF16), w_ple_proj.astype(BF16)
    wi_b = jnp.swapaxes(w_in, 1, 2).astype(BF16)
    for i in range(depth):
        proj, nar = _inproj(xf, norm_mix[i], wi_b, i, tm, _col_tile(COL_NARROW, 1024))
        o = _delta(proj, nar, conv_qkv[i].astype(F32), a_log[i], dt_bias[i], head_norm[i].astype(F32), batch, seq)
        s = _sgu(proj, sgu_norm[i].astype(F32), w_spatial[i].astype(F32), b_spatial[i], _row_tile(seq, 512))
        merged = _merge(o, s, proj, wa_b, wb_b, i, tm, 512)
        xf = _mm_res(merged, wo_b, i, xf, tm_res, d, "out_proj")
        act = _ffn_up(xf, norm_ffn[i].astype(F32), wu_b, i, conv_ffn[i].astype(F32), b_conv_ffn[i], seq, tm,
                      _col_tile(D_FF, 512))
        xf = _mm_res(act, wd_b, i, xf, tm, 512, "ffn_down")
        xf = _ple(xf, norm_ple[i].astype(F32), wg_b, p3, i, wp_b, norm_final.astype(F32), i == depth - 1, tm_res)
    return xf.reshape(batch, seq, d).astype(x.dtype)
```

```python
import functools

import jax
import jax.numpy as jnp
from jax import lax
from jax.experimental import pallas as pl
from jax.experimental.pallas import tpu as pltpu

F32 = jnp.float32
BF16 = jnp.bfloat16
EPS = 1e-6

HEAD_DIM = 128
N_QK_HEADS = 8
N_V_HEADS = 16
QK_WIDTH = N_QK_HEADS * HEAD_DIM
V_WIDTH = N_V_HEADS * HEAD_DIM
CONV_A = 4
CHUNK_A = 64
N_GROUPS_B = 8
GROUP_DIM_B = 128
WIDTH_B = N_GROUPS_B * GROUP_DIM_B
CHUNK_B = 128
D_FF = 5632
CONV_FFN = 3

COL_Q = 0
COL_K = QK_WIDTH
COL_V = 2 * QK_WIDTH
COL_Z = COL_V + V_WIDTH
COL_NARROW = COL_Z + V_WIDTH
N_NARROW = 2 * N_V_HEADS
COL_U = COL_Z + V_WIDTH
COL_VB = COL_U + WIDTH_B
COL_GA = COL_VB + WIDTH_B
N_MAIN = COL_GA + 2 * 2048
LANES = 128
HALO = 8

V7X_VMEM_LIMIT_BYTES = 52 * 1024 * 1024


def _params(sem):
    return pltpu.CompilerParams(dimension_semantics=sem, vmem_limit_bytes=V7X_VMEM_LIMIT_BYTES)


def _row_tile(s, target):
    t = min(target, s)
    while s % t:
        t //= 2
    return t


def _col_tile(n, target):
    t = min(target, n)
    while n % t or t % LANES:
        t -= LANES
    return t


def _rms_rows_to(x_ref, gain_ref, h_ref, rows=32):
    tm = x_ref.shape[0]

    def body(r, carry):
        sl = pl.ds(pl.multiple_of(r * rows, rows), rows)
        x = x_ref[sl, :]
        ms = jnp.mean(x * x, axis=-1, keepdims=True)
        h_ref[sl, :] = ((x * lax.rsqrt(ms + EPS)) * gain_ref[...]).astype(h_ref.dtype)
        return carry

    lax.fori_loop(0, tm // rows, body, 0, unroll=4)


_NT = (((1,), (1,)), ((), ()))


def _inproj_kernel(x_ref, gain_ref, w_ref, wn_ref, out_ref, nar_ref, h_ref):
    @pl.when((pl.program_id(1) == 0) & (pl.program_id(2) == 0))
    def _():
        _rms_rows_to(x_ref, gain_ref, h_ref)
        nar_ref[...] = lax.dot_general(h_ref[...], wn_ref[...], _NT, preferred_element_type=F32)

    out_ref[...] = lax.dot_general(h_ref[...], w_ref[...], _NT, preferred_element_type=F32).astype(out_ref.dtype)


def _inproj(xf, gain, wt_all, layer, tm, tn):
    m, d = xf.shape
    n_in = wt_all.shape[1]
    nj = COL_NARROW // tn
    assert COL_NARROW % tn == 0 and N_MAIN == 2 * COL_NARROW
    wt2 = wt_all.reshape(-1, d)
    base = layer * n_in
    q = N_NARROW
    assert n_in % q == 0 and COL_NARROW % q == 0 and tn % q == 0
    return pl.pallas_call(
        _inproj_kernel,
        grid=(m // tm, 2, nj),
        in_specs=[
            pl.BlockSpec((tm, d), lambda i, g, j: (i, 0)),
            pl.BlockSpec((1, d), lambda i, g, j: (0, 0)),
            pl.BlockSpec((pl.Element(tn), pl.Element(d)),
                         lambda i, g, j: ((base // q + g * ((COL_NARROW + N_NARROW) // q) + j * (tn // q)) * q, 0)),
            pl.BlockSpec((pl.Element(LANES), pl.Element(d)), lambda i, g, j: (base + COL_NARROW, 0)),
        ],
        out_specs=[
            pl.BlockSpec((tm, tn), lambda i, g, j: (i, g * nj + j)),
            pl.BlockSpec((tm, LANES), lambda i, g, j: (i, 0)),
        ],
        out_shape=[jax.ShapeDtypeStruct((m, N_MAIN), BF16), jax.ShapeDtypeStruct((m, LANES), F32)],
        scratch_shapes=[pltpu.VMEM((tm, d), BF16)],
        compiler_params=_params(("parallel", "arbitrary", "arbitrary")),
        name="inproj",
    )(xf, gain.reshape(1, d), wt2, wt2)


def _softplus(x):
    return jnp.maximum(x, 0.0) + jnp.log(1.0 + jnp.exp(-jnp.abs(x)))


def _delta_kernel(nh, group, q_ref, k_ref, v_ref, z_ref, nar_ref, cq_ref, ck_ref, cv_ref, alog_ref, dtb_ref, hg_ref,
                  o_ref, s_ref, cs_ref, lhs_ref, kt_ref, bd_ref, dc_ref, x0_ref, wq_ref, u_ref, at_ref, kd_ref, cd_ref):
    c = CHUNK_A
    hd = HEAD_DIM
    n_groups = q_ref.shape[0] // (group * c)
    hq0 = pl.program_id(1) * nh
    for ref in (s_ref, cs_ref, lhs_ref, kt_ref, bd_ref, dc_ref, x0_ref, wq_ref, u_ref, at_ref, kd_ref, cd_ref):
        ref[...] = jnp.zeros_like(ref)

    row = lax.broadcasted_iota(jnp.int32, (2 * c, 2 * c), 0)
    col = lax.broadcasted_iota(jnp.int32, (2 * c, 2 * c), 1)
    same_head = (row < c) == (col < c)
    causal = same_head & (row >= col)
    strict = same_head & (row > col)
    lane_lo = col < c
    lane_m = col - jnp.where(row < c, 0, 1)
    row_m = row - jnp.where(lane_lo, 0, 1)
    row_c = lax.broadcasted_iota(jnp.int32, (c, LANES), 0)
    neg_exp_alog = -jnp.exp(alog_ref[...])
    q_scale = HEAD_DIM ** -0.5
    heads = [(hh, h) for hh in range(nh) for h in range(2)]

    def dots(lhs_list, rhs_list):
        return [jnp.dot(a, b, preferred_element_type=F32) for a, b in zip(lhs_list, rhs_list)]

    def gates(ci):
        rows = pl.ds(pl.multiple_of(ci * c, c), c)
        nar = nar_ref[rows, :]
        sig = jax.nn.sigmoid(nar)
        gam = neg_exp_alog * _softplus(nar + dtb_ref[...])
        for sh in (1, 2, 4, 8, 16, 32):
            gam = gam + jnp.where(row_c >= sh, pltpu.roll(gam, sh, axis=0), 0.0)
        sig2 = jnp.concatenate([sig, sig], axis=0)
        gam2 = jnp.concatenate([gam, gam], axis=0)
        return sig2, gam2, gam2.T

    def conv_group(c0, hh):
        n = group * c
        rows = pl.ds(pl.multiple_of(c0 * c, c), n)
        srcs = ((q_ref, cq_ref, hh * hd), (k_ref, ck_ref, hh * hd),
                (v_ref, cv_ref, hh * 2 * hd), (v_ref, cv_ref, hh * 2 * hd + hd))
        outs = []
        for s, (x_ref, w_ref, c_lo) in enumerate(srcs):
            cols = slice(c_lo, c_lo + hd)
            cs_ref[hh, s, HALO:HALO + n, :] = x_ref[rows, cols].astype(F32)
            per_chunk = []
            for g in range(group):
                acc = w_ref[CONV_A - 1:CONV_A, cols] * cs_ref[hh, s, HALO + g * c:HALO + (g + 1) * c, :]
                for j in range(CONV_A - 1):
                    off = HALO - (CONV_A - 1) + j + g * c
                    acc = acc + w_ref[j:j + 1, cols] * cs_ref[hh, s, pl.ds(off, c, stride=1), :]
                per_chunk.append(jax.nn.silu(acc))
            outs.append(per_chunk)
            cs_ref[hh, s, 0:HALO, :] = cs_ref[hh, s, n:n + HALO, :]
        return outs

    def prep(grp, par, ring):
        c0 = grp * group
        conv = [conv_group(c0, hh) for hh in range(nh)]
        for g in range(group):
            sig2, gam2, gam2t = gates(c0 + g)
            for hh in range(nh):
                hq2 = 2 * (hq0 + hh)
                qc, kc, v_lo, v_hi = (conv[hh][s][g] for s in range(4))
                qn = qc * (lax.rsqrt(jnp.sum(qc * qc, axis=-1, keepdims=True) + EPS) * q_scale)
                kn = kc * lax.rsqrt(jnp.sum(kc * kc, axis=-1, keepdims=True) + EPS)
                beta_c = jnp.sum(jnp.where(lane_m == hq2, sig2, 0.0), axis=1, keepdims=True)
                gam_c = jnp.sum(jnp.where(lane_m == hq2 + N_V_HEADS, gam2, 0.0), axis=1, keepdims=True)
                gam_r = jnp.sum(jnp.where(row_m == hq2 + N_V_HEADS, gam2t, 0.0), axis=0, keepdims=True)
                gam_cb = jnp.broadcast_to(gam_c, (2 * c, 2 * c))
                decay = jnp.where(causal, jnp.exp(jnp.where(causal, gam_cb - gam_r, 0.0)), 0.0)
                egam_c = jnp.exp(gam_c)
                glast_r = jnp.where(lane_lo[0:1, :], gam_cb[c - 1:c, :], gam_cb[2 * c - 1:2 * c, :])
                kn2 = jnp.concatenate([kn, kn], axis=0)
                kt2 = kn2.T
                kdec = kt2 * jnp.exp(glast_r - gam_r)
                qd2 = jnp.concatenate([qn, qn], axis=0) * egam_c
                slot = ring * group + g
                for h in range(2):
                    keep = lane_lo if h == 0 else jnp.logical_not(lane_lo)
                    kd_ref[slot, hh, h] = jnp.where(keep, kdec, 0.0).astype(BF16)
                    cd = jnp.exp(gam_cb[(h + 1) * c - 1:(h + 1) * c, :])
                    cd_ref[slot, hh, h] = jnp.broadcast_to(cd, (HALO, LANES))
                    wq_ref[slot, hh, h, c:2 * c, :] = qd2[h * c:(h + 1) * c].astype(BF16)
                v2 = jnp.concatenate([v_lo, v_hi], axis=0)
                lhs_ref[par, g, hh] = jnp.concatenate([qn, kn], axis=0).astype(BF16)
                kt_ref[par, g, hh] = kt2.astype(BF16)
                dc_ref[par, g, hh] = decay
                bd_ref[par, g, hh] = jnp.where(strict, beta_c * decay, 0.0)
                x0_ref[par, g, hh] = jnp.concatenate([v2 * beta_c, kn2 * (beta_c * egam_c)], axis=1)

    def solve(par, ring):
        items = [(g, hh) for g in range(group) for hh in range(nh)]
        rs = dots([lhs_ref[par, g, hh] for g, hh in items], [kt_ref[par, g, hh] for g, hh in items])
        yield
        pws, xs = [], []
        for (g, hh), r in zip(items, rs):
            qk2 = jnp.concatenate([r[:c], r[:c]], axis=0)
            kk2 = jnp.concatenate([r[c:], r[c:]], axis=0)
            at_ref[ring * group + g, hh] = (qk2 * dc_ref[par, g, hh]).astype(BF16)
            pws.append((-(kk2 * bd_ref[par, g, hh])).astype(BF16))
            xs.append(x0_ref[par, g, hh])
        for level in range(6):
            if level:
                pws = [p.astype(BF16) for p in dots(pws, pws)]
            xs = [x + d for x, d in zip(xs, dots(pws, [x.astype(BF16) for x in xs]))]
            yield
        for (g, hh), x in zip(items, xs):
            slot = ring * group + g
            u_ref[slot, hh] = x[:, :hd]
            for h in range(2):
                wq_ref[slot, hh, h, 0:c, :] = x[h * c:(h + 1) * c, hd:].astype(BF16)

    def scan(grp, ring):
        for g in range(group):
            slot = ring * group + g
            rows = pl.ds(pl.multiple_of((grp * group + g) * c, c), c)
            res = dots([wq_ref[slot, hh, h] for hh, h in heads], [s_ref[hh, h].astype(BF16) for hh, h in heads])
            yield
            vnew_b, o_state = [], []
            for hh in range(nh):
                u = u_ref[slot, hh]
                r0, r1 = res[2 * hh], res[2 * hh + 1]
                vnew_b.append(jnp.concatenate([u[:c] - r0[:c], u[c:] - r1[:c]], axis=0).astype(BF16))
                o_state.append(jnp.concatenate([r0[c:], r1[c:]], axis=0))
            upd = dots([kd_ref[slot, hh, h] for hh, h in heads], [vnew_b[hh] for hh, h in heads])
            o_intra = dots([at_ref[slot, hh] for hh in range(nh)], vnew_b)
            yield
            for (hh, h), d in zip(heads, upd):
                s_ref[hh, h] = s_ref[hh, h] * cd_ref[slot, hh, h][0:1, :] + d
            for hh in range(nh):
                o2 = o_state[hh] + o_intra[hh]
                zc = z_ref[rows, hh * 2 * hd:(hh + 1) * 2 * hd].astype(F32)
                z2 = jnp.concatenate([zc[:, :hd], zc[:, hd:]], axis=0)
                y = (o2 * lax.rsqrt(jnp.mean(o2 * o2, axis=-1, keepdims=True) + EPS)) * hg_ref[...]
                y = y * jax.nn.silu(z2)
                o_ref[rows, hh * 2 * hd:hh * 2 * hd + hd] = y[:c].astype(o_ref.dtype)
                o_ref[rows, hh * 2 * hd + hd:(hh + 1) * 2 * hd] = y[c:].astype(o_ref.dtype)

    def body(t, carry):
        last = n_groups - 1
        scan_it = scan(jnp.clip(t - 2, 0, last), lax.rem(t + 1, 3))
        solve_it = solve(lax.rem(t + 1, 2), lax.rem(t + 2, 3))
        for turn in range(2 * group + 7):
            if turn % 2 == 0 or turn >= 4 * group:
                next(solve_it, None)
            else:
                next(scan_it, None)
        for _ in scan_it:
            pass
        for _ in solve_it:
            pass
        prep(jnp.minimum(t, last), lax.rem(t, 2), lax.rem(t, 3))
        return carry

    lax.fori_loop(0, n_groups + 2, body, 0)


def _delta(proj, nar, conv_w, a_log, dt_bias, head_gain, batch, seq, nh=2, group=2):
    m = proj.shape[0]
    hd = HEAD_DIM
    c2 = 2 * CHUNK_A
    assert N_QK_HEADS % nh == 0 and seq % (group * CHUNK_A) == 0
    pad = lambda v: jnp.zeros((1, LANES), F32).at[0, N_V_HEADS:2 * N_V_HEADS].set(v.astype(F32))
    qw, vw = nh * hd, nh * 2 * hd
    qb, kb, vb, zb = COL_Q // qw, COL_K // qw, COL_V // vw, COL_Z // vw
    ring = 3 * group
    return pl.pallas_call(
        functools.partial(_delta_kernel, nh, group),
        grid=(batch, N_QK_HEADS // nh),
        in_specs=[
            pl.BlockSpec((seq, qw), lambda b, h: (b, qb + h)),
            pl.BlockSpec((seq, qw), lambda b, h: (b, kb + h)),
            pl.BlockSpec((seq, vw), lambda b, h: (b, vb + h)),
            pl.BlockSpec((seq, vw), lambda b, h: (b, zb + h)),
            pl.BlockSpec((seq, LANES), lambda b, h: (b, 0)),
            pl.BlockSpec((CONV_A, qw), lambda b, h: (0, qb + h)),
            pl.BlockSpec((CONV_A, qw), lambda b, h: (0, kb + h)),
            pl.BlockSpec((CONV_A, vw), lambda b, h: (0, vb + h)),
            pl.BlockSpec((1, LANES), lambda b, h: (0, 0)),
            pl.BlockSpec((1, LANES), lambda b, h: (0, 0)),
            pl.BlockSpec((1, hd), lambda b, h: (0, 0)),
        ],
        out_specs=pl.BlockSpec((seq, vw), lambda b, h: (b, h)),
        out_shape=jax.ShapeDtypeStruct((m, V_WIDTH), BF16),
        scratch_shapes=[
            pltpu.VMEM((nh, 2, hd, hd), F32),
            pltpu.VMEM((nh, 4, 2 * HALO + group * CHUNK_A, LANES), F32),
            pltpu.VMEM((2, group, nh, c2, hd), BF16),
            pltpu.VMEM((2, group, nh, hd, c2), BF16),
            pltpu.VMEM((2, group, nh, c2, c2), F32),
            pltpu.VMEM((2, group, nh, c2, c2), F32),
            pltpu.VMEM((2, group, nh, c2, 2 * hd), F32),
            pltpu.VMEM((ring, nh, 2, c2, hd), BF16),
            pltpu.VMEM((ring, nh, c2, hd), F32),
            pltpu.VMEM((ring, nh, c2, c2), BF16),
            pltpu.VMEM((ring, nh, 2, hd, c2), BF16),
            pltpu.VMEM((ring, nh, 2, HALO, LANES), F32),
        ],
        compiler_params=_params(("parallel", "parallel")),
        name="delta",
    )(proj, proj, proj, proj, nar, conv_w, conv_w, conv_w, pad(a_log), pad(dt_bias), head_gain.reshape(1, hd))


def _sgu_kernel(u_ref, v_ref, gain_ref, w_ref, b_ref, o_ref):
    t = CHUNK_B
    g_dim = GROUP_DIM_B
    row = lax.broadcasted_iota(jnp.int32, (t, t), 0)
    col = lax.broadcasted_iota(jnp.int32, (t, t), 1)
    keep = row >= col

    def chunk(ci, carry):
        rows = pl.ds(pl.multiple_of(ci * t, t), t)
        v = jax.nn.gelu(v_ref[rows, :].astype(F32))
        ms = jnp.mean(v * v, axis=-1, keepdims=True)
        vn = ((v * lax.rsqrt(ms + EPS)) * gain_ref[...]).astype(BF16)
        for g in range(N_GROUPS_B):
            cols = slice(g * g_dim, (g + 1) * g_dim)
            wm = jnp.where(keep, w_ref[g], 0.0).astype(BF16)
            mixed = jnp.dot(wm, vn[:, cols], preferred_element_type=F32) + b_ref[g]
            u = jax.nn.gelu(u_ref[rows, cols].astype(F32))
            o_ref[rows, cols] = (u * mixed).astype(o_ref.dtype)
        return carry

    lax.fori_loop(0, u_ref.shape[0] // t, chunk, 0)


def _sgu(proj, gain, w_s, b_s, tr):
    m = proj.shape[0]
    wb = WIDTH_B
    b_bc = jnp.broadcast_to(b_s.astype(F32)[:, :, None], (N_GROUPS_B, CHUNK_B, GROUP_DIM_B))
    return pl.pallas_call(
        _sgu_kernel,
        grid=(m // tr,),
        in_specs=[
            pl.BlockSpec((tr, wb), lambda i: (i, COL_U // wb)),
            pl.BlockSpec((tr, wb), lambda i: (i, COL_VB // wb)),
            pl.BlockSpec((1, wb), lambda i: (0, 0)),
            pl.BlockSpec((N_GROUPS_B, CHUNK_B, CHUNK_B), lambda i: (0, 0, 0)),
            pl.BlockSpec((N_GROUPS_B, CHUNK_B, GROUP_DIM_B), lambda i: (0, 0, 0)),
        ],
        out_specs=pl.BlockSpec((tr, wb), lambda i: (i, 0)),
        out_shape=jax.ShapeDtypeStruct((m, wb), BF16),
        compiler_params=_params(("parallel",)),
        name="sgu",
    )(proj, proj, gain.reshape(1, wb), w_s, b_bc)


def _merge_kernel(o_ref, s_ref, ga_ref, gb_ref, wa_ref, wb_ref, out_ref):
    tm = out_ref.shape[0]
    hb = tm // 2
    for r in range(2):
        rows = slice(r * hb, (r + 1) * hb)
        ya = jnp.dot(o_ref[rows, :], wa_ref[...], preferred_element_type=F32)
        yb = jnp.dot(s_ref[rows, :], wb_ref[...], preferred_element_type=F32)
        ga = jax.nn.sigmoid(ga_ref[rows, :].astype(F32))
        gb = jax.nn.sigmoid(gb_ref[rows, :].astype(F32))
        out_ref[rows, :] = (ga * ya + gb * yb).astype(out_ref.dtype)


def _merge(o, s, proj, wa, wb, layer, tm, tn):
    m = o.shape[0]
    d = wa.shape[2]
    ga0 = COL_GA // tn
    gb0 = (COL_GA + d) // tn
    return pl.pallas_call(
        _merge_kernel,
        grid=(m // tm, d // tn),
        in_specs=[
            pl.BlockSpec((tm, o.shape[1]), lambda i, j: (i, 0)),
            pl.BlockSpec((tm, s.shape[1]), lambda i, j: (i, 0)),
            pl.BlockSpec((tm, tn), lambda i, j: (i, ga0 + j)),
            pl.BlockSpec((tm, tn), lambda i, j: (i, gb0 + j)),
            pl.BlockSpec((None, wa.shape[1], tn), lambda i, j: (layer, 0, j)),
            pl.BlockSpec((None, wb.shape[1], tn), lambda i, j: (layer, 0, j)),
        ],
        out_specs=pl.BlockSpec((tm, tn), lambda i, j: (i, j)),
        out_shape=jax.ShapeDtypeStruct((m, d), BF16),
        compiler_params=_params(("parallel", "arbitrary")),
        name="merge",
    )(o, s, proj, proj, wa, wb)


def _mm_res_kernel(a_ref, w_ref, res_ref, out_ref):
    tm = out_ref.shape[0]
    hb = tm // 2
    for r in range(2):
        rows = slice(r * hb, (r + 1) * hb)
        out_ref[rows, :] = res_ref[rows, :] + jnp.dot(a_ref[rows, :], w_ref[...], preferred_element_type=F32)


def _resident(block_shape, index_map, n_blocks):
    if n_blocks == 1:
        return pl.BlockSpec(block_shape, index_map, pipeline_mode=pl.Buffered(1))
    return pl.BlockSpec(block_shape, index_map)


def _mm_res(a, w, layer, res, tm, tn, name):
    m, k = a.shape
    n = w.shape[2]
    return pl.pallas_call(
        _mm_res_kernel,
        grid=(m // tm, n // tn),
        in_specs=[
            pl.BlockSpec((tm, k), lambda i, j: (i, 0)),
            _resident((None, k, tn), lambda i, j: (layer, 0, j), n // tn),
            pl.BlockSpec((tm, tn), lambda i, j: (i, j)),
        ],
        out_specs=pl.BlockSpec((tm, tn), lambda i, j: (i, j)),
        out_shape=jax.ShapeDtypeStruct((m, n), F32),
        compiler_params=_params(("parallel", "arbitrary")),
        name=name,
    )(a, w, res)


def _ffn_up_kernel(blocks_per_seq, x_ref, gain_ref, wg_ref, wv_ref, cg_ref, cv_ref, bg_ref, bv_ref, out_ref,
                   h_ref, ug_ref, uv_ref, tg_ref, tv_ref):
    i = pl.program_id(0)
    j = pl.program_id(1)
    tm, tn = out_ref.shape
    half = min(tn, 2 * LANES)
    sub = 128

    @pl.when(j == 0)
    def _():
        _rms_rows_to(x_ref, gain_ref, h_ref)

    first = (i % blocks_per_seq) == 0
    h = h_ref[...]
    for c0 in range(0, tn, half):
        slabs = range(c0 // LANES, (c0 + half) // LANES)
        for w_ref, u_ref, t_ref in ((wg_ref, ug_ref, tg_ref), (wv_ref, uv_ref, tv_ref)):
            res = jnp.dot(h, w_ref[:, c0:c0 + half], preferred_element_type=F32)
            for s in slabs:
                u_ref[s, 0:HALO, :] = jnp.where(first, 0.0, t_ref[j, s])
                u_ref[s, HALO:, :] = res[:, s * LANES - c0:(s + 1) * LANES - c0]
                t_ref[j, s] = u_ref[s, tm:tm + HALO, :]

        def conv(u_ref, c_ref, b_ref, s, r):
            cols = slice(s * LANES, (s + 1) * LANES)
            acc = b_ref[:, cols] + c_ref[CONV_FFN - 1:CONV_FFN, cols] * u_ref[s, HALO + r * sub:HALO + (r + 1) * sub, :]
            for t in range(CONV_FFN - 1):
                off = HALO - (CONV_FFN - 1) + t + r * sub
                acc = acc + c_ref[t:t + 1, cols] * u_ref[s, pl.ds(off, sub, stride=1), :]
            return acc

        for s in slabs:
            for r in range(tm // sub):
                gate = conv(ug_ref, cg_ref, bg_ref, s, r)
                val = conv(uv_ref, cv_ref, bv_ref, s, r)
                out_ref[r * sub:(r + 1) * sub, s * LANES:(s + 1) * LANES] = (jax.nn.silu(gate) * val).astype(out_ref.dtype)


def _ffn_up(xf, gain, w_up, layer, conv_w, conv_b, seq, tm, tn):
    m, d = xf.shape
    nj = D_FF // tn
    b2 = conv_b.reshape(1, 2 * D_FF).astype(F32)
    return pl.pallas_call(
        functools.partial(_ffn_up_kernel, seq // tm),
        grid=(m // tm, nj),
        in_specs=[
            pl.BlockSpec((tm, d), lambda i, j: (i, 0)),
            pl.BlockSpec((1, d), lambda i, j: (0, 0)),
            pl.BlockSpec((None, d, tn), lambda i, j: (layer, 0, j)),
            pl.BlockSpec((None, d, tn), lambda i, j: (layer, 0, nj + j)),
            pl.BlockSpec((CONV_FFN, tn), lambda i, j: (0, j)),
            pl.BlockSpec((CONV_FFN, tn), lambda i, j: (0, nj + j)),
            pl.BlockSpec((1, tn), lambda i, j: (0, j)),
            pl.BlockSpec((1, tn), lambda i, j: (0, nj + j)),
        ],
        out_specs=pl.BlockSpec((tm, tn), lambda i, j: (i, j)),
        out_shape=jax.ShapeDtypeStruct((m, D_FF), BF16),
        scratch_shapes=[
            pltpu.VMEM((tm, d), BF16),
            pltpu.VMEM((tn // LANES, tm + HALO, LANES), F32),
            pltpu.VMEM((tn // LANES, tm + HALO, LANES), F32),
            pltpu.VMEM((nj, tn // LANES, HALO, LANES), F32),
            pltpu.VMEM((nj, tn // LANES, HALO, LANES), F32),
        ],
        compiler_params=_params(("arbitrary", "arbitrary")),
        name="ffn_up",
    )(xf, gain.reshape(1, d), w_up, w_up, conv_w, conv_w, b2, b2)


def _ple_kernel(last, x_ref, gain_ref, wg_ref, p_ref, wp_ref, fgain_ref, out_ref, h_ref):
    tm = out_ref.shape[0]
    hb = tm // 2
    _rms_rows_to(x_ref, gain_ref, h_ref)
    for r in range(2):
        rows = slice(r * hb, (r + 1) * hb)
        gate = jax.nn.sigmoid(jnp.dot(h_ref[rows, :], wg_ref[...], preferred_element_type=F32))
        emb = jnp.dot(p_ref[rows, :].astype(BF16), wp_ref[...], preferred_element_type=F32)
        y = x_ref[rows, :] + gate * emb
        if last:
            y = (y * lax.rsqrt(jnp.mean(y * y, axis=-1, keepdims=True) + EPS)) * fgain_ref[...]
        out_ref[rows, :] = y


def _ple(xf, gain, wg, p3, layer, wp, final_gain, last, tm):
    m, d = xf.shape
    pd = p3.shape[2]
    return pl.pallas_call(
        functools.partial(_ple_kernel, last),
        grid=(m // tm,),
        in_specs=[
            pl.BlockSpec((tm, d), lambda i: (i, 0)),
            pl.BlockSpec((1, d), lambda i: (0, 0)),
            _resident((None, d, d), lambda i: (layer, 0, 0), 1),
            pl.BlockSpec((None, tm, pd), lambda i: (layer, i, 0)),
            _resident((None, pd, d), lambda i: (layer, 0, 0), 1),
            pl.BlockSpec((1, d), lambda i: (0, 0)),
        ],
        out_specs=pl.BlockSpec((tm, d), lambda i: (i, 0)),
        out_shape=jax.ShapeDtypeStruct((m, d), F32),
        scratch_shapes=[pltpu.VMEM((tm, d), BF16)],
        compiler_params=_params(("parallel",)),
        name="ple",
    )(xf, gain.reshape(1, d), wg, p3, wp, final_gain.reshape(1, d))


def kernel(x, p, norm_mix, w_in, conv_qkv, a_log, dt_bias, head_norm, sgu_norm, w_spatial, b_spatial, w_branch_a, w_branch_b, w_out, norm_ffn, w_ffn_up, conv_ffn, b_conv_ffn, w_ffn_down, norm_ple, w_ple_gate, w_ple_proj, norm_final):
    batch, seq, d = x.shape
    depth = w_in.shape[0]
    m = batch * seq
    assert d == V_WIDTH and seq % CHUNK_B == 0 and w_in.shape[2] == N_MAIN + N_NARROW
    tm = _row_tile(seq, 1024)
    tm_res = _row_tile(seq, 512)
    xf = x.reshape(m, d).astype(F32)
    p3 = p.reshape(depth, m, p.shape[-1])
    wa_b, wb_b, wo_b = w_branch_a.astype(BF16), w_branch_b.astype(BF16), w_out.astype(BF16)
    wu_b, wd_b = w_ffn_up.astype(BF16), w_ffn_down.astype(BF16)
    wg_b, wp_b = w_ple_gate.astype(BF16), w_ple_proj.astype(BF16)
    wi_b = jnp.swapaxes(w_in, 1, 2).astype(BF16)
    for i in range(depth):
        proj, nar = _inproj(xf, norm_mix[i], wi_b, i, tm, _col_tile(COL_NARROW, 1024))
        o = _delta(proj, nar, conv_qkv[i].astype(F32), a_log[i], dt_bias[i], head_norm[i].astype(F32), batch, seq)
        s = _sgu(proj, sgu_norm[i].astype(F32), w_spatial[i].astype(F32), b_spatial[i], _row_tile(seq, 512))
        merged = _merge(o, s, proj, wa_b, wb_b, i, tm, 512)
        xf = _mm_res(merged, wo_b, i, xf, tm_res, d, "out_proj")
        act = _ffn_up(xf, norm_ffn[i].astype(F32), wu_b, i, conv_ffn[i].astype(F32), b_conv_ffn[i], seq, tm,
                      _col_tile(D_FF, 512))
        xf = _mm_res(act, wd_b, i, xf, tm, 512, "ffn_down")
        xf = _ple(xf, norm_ple[i].astype(F32), wg_b, p3, i, wp_b, norm_final.astype(F32), i == depth - 1, tm_res)
    return xf.reshape(batch, seq, d).astype(x.dtype)
```

```python
import functools

import jax
import jax.numpy as jnp
from jax import lax
from jax.experimental import pallas as pl
from jax.experimental.pallas import tpu as pltpu

F32 = jnp.float32
BF16 = jnp.bfloat16
EPS = 1e-6

HEAD_DIM = 128
N_QK_HEADS = 8
N_V_HEADS = 16
QK_WIDTH = N_QK_HEADS * HEAD_DIM
V_WIDTH = N_V_HEADS * HEAD_DIM
CONV_A = 4
CHUNK_A = 64
N_GROUPS_B = 8
GROUP_DIM_B = 128
WIDTH_B = N_GROUPS_B * GROUP_DIM_B
CHUNK_B = 128
D_FF = 5632
CONV_FFN = 3

COL_Q = 0
COL_K = QK_WIDTH
COL_V = 2 * QK_WIDTH
COL_Z = COL_V + V_WIDTH
COL_NARROW = COL_Z + V_WIDTH
N_NARROW = 2 * N_V_HEADS
COL_U = COL_Z + V_WIDTH
COL_VB = COL_U + WIDTH_B
COL_GA = COL_VB + WIDTH_B
N_MAIN = COL_GA + 2 * 2048
LANES = 128
HALO = 8

V7X_VMEM_LIMIT_BYTES = 52 * 1024 * 1024


def _params(sem):
    return pltpu.CompilerParams(dimension_semantics=sem, vmem_limit_bytes=V7X_VMEM_LIMIT_BYTES)


def _row_tile(s, target):
    t = min(target, s)
    while s % t:
        t //= 2
    return t


def _col_tile(n, target):
    t = min(target, n)
    while n % t or t % LANES:
        t -= LANES
    return t


def _rms_rows_to(x_ref, gain_ref, h_ref, rows=32):
    tm = x_ref.shape[0]

    def body(r, carry):
        sl = pl.ds(pl.multiple_of(r * rows, rows), rows)
        x = x_ref[sl, :]
        ms = jnp.mean(x * x, axis=-1, keepdims=True)
        h_ref[sl, :] = ((x * lax.rsqrt(ms + EPS)) * gain_ref[...]).astype(h_ref.dtype)
        return carry

    lax.fori_loop(0, tm // rows, body, 0, unroll=4)


_NT = (((1,), (1,)), ((), ()))


def _inproj_kernel(x_ref, gain_ref, w_ref, wn_ref, out_ref, nar_ref, h_ref):
    @pl.when((pl.program_id(1) == 0) & (pl.program_id(2) == 0))
    def _():
        _rms_rows_to(x_ref, gain_ref, h_ref)
        nar_ref[...] = lax.dot_general(h_ref[...], wn_ref[...], _NT, preferred_element_type=F32)

    out_ref[...] = lax.dot_general(h_ref[...], w_ref[...], _NT, preferred_element_type=F32).astype(out_ref.dtype)


def _inproj(xf, gain, wt_all, layer, tm, tn):
    m, d = xf.shape
    n_in = wt_all.shape[1]
    nj = COL_NARROW // tn
    assert COL_NARROW % tn == 0 and N_MAIN == 2 * COL_NARROW
    wt2 = wt_all.reshape(-1, d)
    base = layer * n_in
    q = N_NARROW
    assert n_in % q == 0 and COL_NARROW % q == 0 and tn % q == 0
    return pl.pallas_call(
        _inproj_kernel,
        grid=(m // tm, 2, nj),
        in_specs=[
            pl.BlockSpec((tm, d), lambda i, g, j: (i, 0)),
            pl.BlockSpec((1, d), lambda i, g, j: (0, 0)),
            pl.BlockSpec((pl.Element(tn), pl.Element(d)),
                         lambda i, g, j: ((base // q + g * ((COL_NARROW + N_NARROW) // q) + j * (tn // q)) * q, 0)),
            pl.BlockSpec((pl.Element(LANES), pl.Element(d)), lambda i, g, j: (base + COL_NARROW, 0)),
        ],
        out_specs=[
            pl.BlockSpec((tm, tn), lambda i, g, j: (i, g * nj + j)),
            pl.BlockSpec((tm, LANES), lambda i, g, j: (i, 0)),
        ],
        out_shape=[jax.ShapeDtypeStruct((m, N_MAIN), BF16), jax.ShapeDtypeStruct((m, LANES), F32)],
        scratch_shapes=[pltpu.VMEM((tm, d), BF16)],
        compiler_params=_params(("parallel", "arbitrary", "arbitrary")),
        name="inproj",
    )(xf, gain.reshape(1, d), wt2, wt2)


def _softplus(x):
    return jnp.maximum(x, 0.0) + jnp.log(1.0 + jnp.exp(-jnp.abs(x)))


def _delta_kernel(nh, group, q_ref, k_ref, v_ref, z_ref, nar_ref, cq_ref, ck_ref, cv_ref, alog_ref, dtb_ref, hg_ref,
                  o_ref, s_ref, cs_ref, lhs_ref, kt_ref, bd_ref, dc_ref, x0_ref, wq_ref, u_ref, at_ref, kd_ref, cd_ref):
    c = CHUNK_A
    hd = HEAD_DIM
    n_groups = q_ref.shape[0] // (group * c)
    hq0 = pl.program_id(1) * nh
    for ref in (s_ref, cs_ref, lhs_ref, kt_ref, bd_ref, dc_ref, x0_ref, wq_ref, u_ref, at_ref, kd_ref, cd_ref):
        ref[...] = jnp.zeros_like(ref)

    row = lax.broadcasted_iota(jnp.int32, (2 * c, 2 * c), 0)
    col = lax.broadcasted_iota(jnp.int32, (2 * c, 2 * c), 1)
    same_head = (row < c) == (col < c)
    causal = same_head & (row >= col)
    strict = same_head & (row > col)
    lane_lo = col < c
    lane_m = col - jnp.where(row < c, 0, 1)
    row_m = row - jnp.where(lane_lo, 0, 1)
    row_c = lax.broadcasted_iota(jnp.int32, (c, LANES), 0)
    neg_exp_alog = -jnp.exp(alog_ref[...])
    q_scale = HEAD_DIM ** -0.5
    heads = [(hh, h) for hh in range(nh) for h in range(2)]

    def dots(lhs_list, rhs_list):
        return [jnp.dot(a, b, preferred_element_type=F32) for a, b in zip(lhs_list, rhs_list)]

    def gates(ci):
        rows = pl.ds(pl.multiple_of(ci * c, c), c)
        nar = nar_ref[rows, :]
        sig = jax.nn.sigmoid(nar)
        gam = neg_exp_alog * _softplus(nar + dtb_ref[...])
        for sh in (1, 2, 4, 8, 16, 32):
            gam = gam + jnp.where(row_c >= sh, pltpu.roll(gam, sh, axis=0), 0.0)
        sig2 = jnp.concatenate([sig, sig], axis=0)
        gam2 = jnp.concatenate([gam, gam], axis=0)
        return sig2, gam2, gam2.T

    def conv_group(c0, hh):
        n = group * c
        rows = pl.ds(pl.multiple_of(c0 * c, c), n)
        srcs = ((q_ref, cq_ref, hh * hd), (k_ref, ck_ref, hh * hd),
                (v_ref, cv_ref, hh * 2 * hd), (v_ref, cv_ref, hh * 2 * hd + hd))
        outs = []
        for s, (x_ref, w_ref, c_lo) in enumerate(srcs):
            cols = slice(c_lo, c_lo + hd)
            cs_ref[hh, s, HALO:HALO + n, :] = x_ref[rows, cols].astype(F32)
            per_chunk = []
            for g in range(group):
                acc = w_ref[CONV_A - 1:CONV_A, cols] * cs_ref[hh, s, HALO + g * c:HALO + (g + 1) * c, :]
                for j in range(CONV_A - 1):
                    off = HALO - (CONV_A - 1) + j + g * c
                    acc = acc + w_ref[j:j + 1, cols] * cs_ref[hh, s, pl.ds(off, c, stride=1), :]
                per_chunk.append(jax.nn.silu(acc))
            outs.append(per_chunk)
            cs_ref[hh, s, 0:HALO, :] = cs_ref[hh, s, n:n + HALO, :]
        return outs

    def prep(grp, par, ring):
        c0 = grp * group
        conv = [conv_group(c0, hh) for hh in range(nh)]
        for g in range(group):
            sig2, gam2, gam2t = gates(c0 + g)
            for hh in range(nh):
                hq2 = 2 * (hq0 + hh)
                qc, kc, v_lo, v_hi = (conv[hh][s][g] for s in range(4))
                qn = qc * (lax.rsqrt(jnp.sum(qc * qc, axis=-1, keepdims=True) + EPS) * q_scale)
                kn = kc * lax.rsqrt(jnp.sum(kc * kc, axis=-1, keepdims=True) + EPS)
                beta_c = jnp.sum(jnp.where(lane_m == hq2, sig2, 0.0), axis=1, keepdims=True)
                gam_c = jnp.sum(jnp.where(lane_m == hq2 + N_V_HEADS, gam2, 0.0), axis=1, keepdims=True)
                gam_r = jnp.sum(jnp.where(row_m == hq2 + N_V_HEADS, gam2t, 0.0), axis=0, keepdims=True)
                gam_cb = jnp.broadcast_to(gam_c, (2 * c, 2 * c))
                decay = jnp.where(causal, jnp.exp(jnp.where(causal, gam_cb - gam_r, 0.0)), 0.0)
                egam_c = jnp.exp(gam_c)
                glast_r = jnp.where(lane_lo[0:1, :], gam_cb[c - 1:c, :], gam_cb[2 * c - 1:2 * c, :])
                kn2 = jnp.concatenate([kn, kn], axis=0)
                kt2 = kn2.T
                kdec = kt2 * jnp.exp(glast_r - gam_r)
                qd2 = jnp.concatenate([qn, qn], axis=0) * egam_c
                slot = ring * group + g
                for h in range(2):
                    keep = lane_lo if h == 0 else jnp.logical_not(lane_lo)
                    kd_ref[slot, hh, h] = jnp.where(keep, kdec, 0.0).astype(BF16)
                    cd = jnp.exp(gam_cb[(h + 1) * c - 1:(h + 1) * c, :])
                    cd_ref[slot, hh, h] = jnp.broadcast_to(cd, (HALO, LANES))
                    wq_ref[slot, hh, h, c:2 * c, :] = qd2[h * c:(h + 1) * c].astype(BF16)
                v2 = jnp.concatenate([v_lo, v_hi], axis=0)
                lhs_ref[par, g, hh] = jnp.concatenate([qn, kn], axis=0).astype(BF16)
                kt_ref[par, g, hh] = kt2.astype(BF16)
                dc_ref[par, g, hh] = decay
                bd_ref[par, g, hh] = jnp.where(strict, beta_c * decay, 0.0)
                x0_ref[par, g, hh] = jnp.concatenate([v2 * beta_c, kn2 * (beta_c * egam_c)], axis=1)

    def solve(par, ring):
        items = [(g, hh) for g in range(group) for hh in range(nh)]
        rs = dots([lhs_ref[par, g, hh] for g, hh in items], [kt_ref[par, g, hh] for g, hh in items])
        yield
        pws, xs = [], []
        for (g, hh), r in zip(items, rs):
            qk2 = jnp.concatenate([r[:c], r[:c]], axis=0)
            kk2 = jnp.concatenate([r[c:], r[c:]], axis=0)
            at_ref[ring * group + g, hh] = (qk2 * dc_ref[par, g, hh]).astype(BF16)
            pws.append((-(kk2 * bd_ref[par, g, hh])).astype(BF16))
            xs.append(x0_ref[par, g, hh])
        for level in range(6):
            if level:
                pws = [p.astype(BF16) for p in dots(pws, pws)]
            xs = [x + d for x, d in zip(xs, dots(pws, [x.astype(BF16) for x in xs]))]
            yield
        for (g, hh), x in zip(items, xs):
            slot = ring * group + g
            u_ref[slot, hh] = x[:, :hd]
            for h in range(2):
                wq_ref[slot, hh, h, 0:c, :] = x[h * c:(h + 1) * c, hd:].astype(BF16)

    def scan(grp, ring):
        for g in range(group):
            slot = ring * group + g
            rows = pl.ds(pl.multiple_of((grp * group + g) * c, c), c)
            res = dots([wq_ref[slot, hh, h] for hh, h in heads], [s_ref[hh, h].astype(BF16) for hh, h in heads])
            yield
            vnew_b, o_state = [], []
            for hh in range(nh):
                u = u_ref[slot, hh]
                r0, r1 = res[2 * hh], res[2 * hh + 1]
                vnew_b.append(jnp.concatenate([u[:c] - r0[:c], u[c:] - r1[:c]], axis=0).astype(BF16))
                o_state.append(jnp.concatenate([r0[c:], r1[c:]], axis=0))
            upd = dots([kd_ref[slot, hh, h] for hh, h in heads], [vnew_b[hh] for hh, h in heads])
            o_intra = dots([at_ref[slot, hh] for hh in range(nh)], vnew_b)
            yield
            for (hh, h), d in zip(heads, upd):
                s_ref[hh, h] = s_ref[hh, h] * cd_ref[slot, hh, h][0:1, :] + d
            for hh in range(nh):
                o2 = o_state[hh] + o_intra[hh]
                zc = z_ref[rows, hh * 2 * hd:(hh + 1) * 2 * hd].astype(F32)
                z2 = jnp.concatenate([zc[:, :hd], zc[:, hd:]], axis=0)
                y = (o2 * lax.rsqrt(jnp.mean(o2 * o2, axis=-1, keepdims=True) + EPS)) * hg_ref[...]
                y = y * jax.nn.silu(z2)
                o_ref[rows, hh * 2 * hd:hh * 2 * hd + hd] = y[:c].astype(o_ref.dtype)
                o_ref[rows, hh * 2 * hd + hd:(hh + 1) * 2 * hd] = y[c:].astype(o_ref.dtype)

    last = n_groups - 1

    def trip(t, r):
        scan_it = scan(jnp.clip(t - 2, 0, last), (r + 1) % 3)
        solve_it = solve((r + 2) % 3, (r + 2) % 3)
        for turn in range(2 * group + 7):
            if turn % 2 == 0 or turn >= 4 * group:
                next(solve_it, None)
            else:
                next(scan_it, None)
        for _ in scan_it:
            pass
        for _ in solve_it:
            pass
        prep(jnp.minimum(t, last), r, r)

    def body(u, carry):
        for r in range(3):
            trip(3 * u + r, r)
        return carry

    n_trips = n_groups + 2
    lax.fori_loop(0, n_trips // 3, body, 0)
    for r in range(n_trips % 3):
        trip((n_trips // 3) * 3 + r, r)


def _delta(proj, nar, conv_w, a_log, dt_bias, head_gain, batch, seq, nh=2, group=2):
    m = proj.shape[0]
    hd = HEAD_DIM
    c2 = 2 * CHUNK_A
    assert N_QK_HEADS % nh == 0 and seq % (group * CHUNK_A) == 0
    pad = lambda v: jnp.zeros((1, LANES), F32).at[0, N_V_HEADS:2 * N_V_HEADS].set(v.astype(F32))
    qw, vw = nh * hd, nh * 2 * hd
    qb, kb, vb, zb = COL_Q // qw, COL_K // qw, COL_V // vw, COL_Z // vw
    ring = 3 * group
    return pl.pallas_call(
        functools.partial(_delta_kernel, nh, group),
        grid=(batch, N_QK_HEADS // nh),
        in_specs=[
            pl.BlockSpec((seq, qw), lambda b, h: (b, qb + h)),
            pl.BlockSpec((seq, qw), lambda b, h: (b, kb + h)),
            pl.BlockSpec((seq, vw), lambda b, h: (b, vb + h)),
            pl.BlockSpec((seq, vw), lambda b, h: (b, zb + h)),
            pl.BlockSpec((seq, LANES), lambda b, h: (b, 0)),
            pl.BlockSpec((CONV_A, qw), lambda b, h: (0, qb + h)),
            pl.BlockSpec((CONV_A, qw), lambda b, h: (0, kb + h)),
            pl.BlockSpec((CONV_A, vw), lambda b, h: (0, vb + h)),
            pl.BlockSpec((1, LANES), lambda b, h: (0, 0)),
            pl.BlockSpec((1, LANES), lambda b, h: (0, 0)),
            pl.BlockSpec((1, hd), lambda b, h: (0, 0)),
        ],
        out_specs=pl.BlockSpec((seq, vw), lambda b, h: (b, h)),
        out_shape=jax.ShapeDtypeStruct((m, V_WIDTH), BF16),
        scratch_shapes=[
            pltpu.VMEM((nh, 2, hd, hd), F32),
            pltpu.VMEM((nh, 4, 2 * HALO + group * CHUNK_A, LANES), F32),
            pltpu.VMEM((3, group, nh, c2, hd), BF16),
            pltpu.VMEM((3, group, nh, hd, c2), BF16),
            pltpu.VMEM((3, group, nh, c2, c2), F32),
            pltpu.VMEM((3, group, nh, c2, c2), F32),
            pltpu.VMEM((3, group, nh, c2, 2 * hd), F32),
            pltpu.VMEM((ring, nh, 2, c2, hd), BF16),
            pltpu.VMEM((ring, nh, c2, hd), F32),
            pltpu.VMEM((ring, nh, c2, c2), BF16),
            pltpu.VMEM((ring, nh, 2, hd, c2), BF16),
            pltpu.VMEM((ring, nh, 2, HALO, LANES), F32),
        ],
        compiler_params=_params(("parallel", "parallel")),
        name="delta",
    )(proj, proj, proj, proj, nar, conv_w, conv_w, conv_w, pad(a_log), pad(dt_bias), head_gain.reshape(1, hd))


def _sgu_kernel(u_ref, v_ref, gain_ref, w_ref, b_ref, o_ref):
    t = CHUNK_B
    g_dim = GROUP_DIM_B
    row = lax.broadcasted_iota(jnp.int32, (t, t), 0)
    col = lax.broadcasted_iota(jnp.int32, (t, t), 1)
    keep = row >= col

    def chunk(ci, carry):
        rows = pl.ds(pl.multiple_of(ci * t, t), t)
        v = jax.nn.gelu(v_ref[rows, :].astype(F32))
        ms = jnp.mean(v * v, axis=-1, keepdims=True)
        vn = ((v * lax.rsqrt(ms + EPS)) * gain_ref[...]).astype(BF16)
        for g in range(N_GROUPS_B):
            cols = slice(g * g_dim, (g + 1) * g_dim)
            wm = jnp.where(keep, w_ref[g], 0.0).astype(BF16)
            mixed = jnp.dot(wm, vn[:, cols], preferred_element_type=F32) + b_ref[g]
            u = jax.nn.gelu(u_ref[rows, cols].astype(F32))
            o_ref[rows, cols] = (u * mixed).astype(o_ref.dtype)
        return carry

    lax.fori_loop(0, u_ref.shape[0] // t, chunk, 0)


def _sgu(proj, gain, w_s, b_s, tr):
    m = proj.shape[0]
    wb = WIDTH_B
    b_bc = jnp.broadcast_to(b_s.astype(F32)[:, :, None], (N_GROUPS_B, CHUNK_B, GROUP_DIM_B))
    return pl.pallas_call(
        _sgu_kernel,
        grid=(m // tr,),
        in_specs=[
            pl.BlockSpec((tr, wb), lambda i: (i, COL_U // wb)),
            pl.BlockSpec((tr, wb), lambda i: (i, COL_VB // wb)),
            pl.BlockSpec((1, wb), lambda i: (0, 0)),
            pl.BlockSpec((N_GROUPS_B, CHUNK_B, CHUNK_B), lambda i: (0, 0, 0)),
            pl.BlockSpec((N_GROUPS_B, CHUNK_B, GROUP_DIM_B), lambda i: (0, 0, 0)),
        ],
        out_specs=pl.BlockSpec((tr, wb), lambda i: (i, 0)),
        out_shape=jax.ShapeDtypeStruct((m, wb), BF16),
        compiler_params=_params(("parallel",)),
        name="sgu",
    )(proj, proj, gain.reshape(1, wb), w_s, b_bc)


def _merge_kernel(o_ref, s_ref, ga_ref, gb_ref, wa_ref, wb_ref, out_ref):
    tm = out_ref.shape[0]
    hb = tm // 2
    for r in range(2):
        rows = slice(r * hb, (r + 1) * hb)
        ya = jnp.dot(o_ref[rows, :], wa_ref[...], preferred_element_type=F32)
        yb = jnp.dot(s_ref[rows, :], wb_ref[...], preferred_element_type=F32)
        ga = jax.nn.sigmoid(ga_ref[rows, :].astype(F32))
        gb = jax.nn.sigmoid(gb_ref[rows, :].astype(F32))
        out_ref[rows, :] = (ga * ya + gb * yb).astype(out_ref.dtype)


def _merge(o, s, proj, wa, wb, layer, tm, tn):
    m = o.shape[0]
    d = wa.shape[2]
    ga0 = COL_GA // tn
    gb0 = (COL_GA + d) // tn
    return pl.pallas_call(
        _merge_kernel,
        grid=(m // tm, d // tn),
        in_specs=[
            pl.BlockSpec((tm, o.shape[1]), lambda i, j: (i, 0)),
            pl.BlockSpec((tm, s.shape[1]), lambda i, j: (i, 0)),
            pl.BlockSpec((tm, tn), lambda i, j: (i, ga0 + j)),
            pl.BlockSpec((tm, tn), lambda i, j: (i, gb0 + j)),
            pl.BlockSpec((None, wa.shape[1], tn), lambda i, j: (layer, 0, j)),
            pl.BlockSpec((None, wb.shape[1], tn), lambda i, j: (layer, 0, j)),
        ],
        out_specs=pl.BlockSpec((tm, tn), lambda i, j: (i, j)),
        out_shape=jax.ShapeDtypeStruct((m, d), BF16),
        compiler_params=_params(("parallel", "arbitrary")),
        name="merge",
    )(o, s, proj, proj, wa, wb)


def _mm_res_kernel(a_ref, w_ref, res_ref, out_ref):
    tm = out_ref.shape[0]
    hb = tm // 2
    for r in range(2):
        rows = slice(r * hb, (r + 1) * hb)
        out_ref[rows, :] = res_ref[rows, :] + jnp.dot(a_ref[rows, :], w_ref[...], preferred_element_type=F32)


def _resident(block_shape, index_map, n_blocks):
    if n_blocks == 1:
        return pl.BlockSpec(block_shape, index_map, pipeline_mode=pl.Buffered(1))
    return pl.BlockSpec(block_shape, index_map)


def _mm_res(a, w, layer, res, tm, tn, name):
    m, k = a.shape
    n = w.shape[2]
    return pl.pallas_call(
        _mm_res_kernel,
        grid=(m // tm, n // tn),
        in_specs=[
            pl.BlockSpec((tm, k), lambda i, j: (i, 0)),
            _resident((None, k, tn), lambda i, j: (layer, 0, j), n // tn),
            pl.BlockSpec((tm, tn), lambda i, j: (i, j)),
        ],
        out_specs=pl.BlockSpec((tm, tn), lambda i, j: (i, j)),
        out_shape=jax.ShapeDtypeStruct((m, n), F32),
        compiler_params=_params(("parallel", "arbitrary")),
        name=name,
    )(a, w, res)


def _ffn_up_kernel(blocks_per_seq, x_ref, gain_ref, wg_ref, wv_ref, cg_ref, cv_ref, bg_ref, bv_ref, out_ref,
                   h_ref, ug_ref, uv_ref, tg_ref, tv_ref):
    i = pl.program_id(0)
    j = pl.program_id(1)
    tm, tn = out_ref.shape
    half = min(tn, 2 * LANES)
    sub = 128

    @pl.when(j == 0)
    def _():
        _rms_rows_to(x_ref, gain_ref, h_ref)

    first = (i % blocks_per_seq) == 0
    h = h_ref[...]
    for c0 in range(0, tn, half):
        slabs = range(c0 // LANES, (c0 + half) // LANES)
        for w_ref, u_ref, t_ref in ((wg_ref, ug_ref, tg_ref), (wv_ref, uv_ref, tv_ref)):
            res = jnp.dot(h, w_ref[:, c0:c0 + half], preferred_element_type=F32)
            for s in slabs:
                u_ref[s, 0:HALO, :] = jnp.where(first, 0.0, t_ref[j, s])
                u_ref[s, HALO:, :] = res[:, s * LANES - c0:(s + 1) * LANES - c0]
                t_ref[j, s] = u_ref[s, tm:tm + HALO, :]

        def conv(u_ref, c_ref, b_ref, s, r):
            cols = slice(s * LANES, (s + 1) * LANES)
            acc = b_ref[:, cols] + c_ref[CONV_FFN - 1:CONV_FFN, cols] * u_ref[s, HALO + r * sub:HALO + (r + 1) * sub, :]
            for t in range(CONV_FFN - 1):
                off = HALO - (CONV_FFN - 1) + t + r * sub
                acc = acc + c_ref[t:t + 1, cols] * u_ref[s, pl.ds(off, sub, stride=1), :]
            return acc

        for s in slabs:
            for r in range(tm // sub):
                gate = conv(ug_ref, cg_ref, bg_ref, s, r)
                val = conv(uv_ref, cv_ref, bv_ref, s, r)
                out_ref[r * sub:(r + 1) * sub, s * LANES:(s + 1) * LANES] = (jax.nn.silu(gate) * val).astype(out_ref.dtype)


def _ffn_up(xf, gain, w_up, layer, conv_w, conv_b, seq, tm, tn):
    m, d = xf.shape
    nj = D_FF // tn
    b2 = conv_b.reshape(1, 2 * D_FF).astype(F32)
    return pl.pallas_call(
        functools.partial(_ffn_up_kernel, seq // tm),
        grid=(m // tm, nj),
        in_specs=[
            pl.BlockSpec((tm, d), lambda i, j: (i, 0)),
            pl.BlockSpec((1, d), lambda i, j: (0, 0)),
            pl.BlockSpec((None, d, tn), lambda i, j: (layer, 0, j)),
            pl.BlockSpec((None, d, tn), lambda i, j: (layer, 0, nj + j)),
            pl.BlockSpec((CONV_FFN, tn), lambda i, j: (0, j)),
            pl.BlockSpec((CONV_FFN, tn), lambda i, j: (0, nj + j)),
            pl.BlockSpec((1, tn), lambda i, j: (0, j)),
            pl.BlockSpec((1, tn), lambda i, j: (0, nj + j)),
        ],
        out_specs=pl.BlockSpec((tm, tn), lambda i, j: (i, j)),
        out_shape=jax.ShapeDtypeStruct((m, D_FF), BF16),
        scratch_shapes=[
            pltpu.VMEM((tm, d), BF16),
            pltpu.VMEM((tn // LANES, tm + HALO, LANES), F32),
            pltpu.VMEM((tn // LANES, tm + HALO, LANES), F32),
            pltpu.VMEM((nj, tn // LANES, HALO, LANES), F32),
            pltpu.VMEM((nj, tn // LANES, HALO, LANES), F32),
        ],
        compiler_params=_params(("arbitrary", "arbitrary")),
        name="ffn_up",
    )(xf, gain.reshape(1, d), w_up, w_up, conv_w, conv_w, b2, b2)


def _ple_kernel(last, x_ref, gain_ref, wg_ref, p_ref, wp_ref, fgain_ref, out_ref, h_ref):
    tm = out_ref.shape[0]
    hb = tm // 2
    _rms_rows_to(x_ref, gain_ref, h_ref)
    for r in range(2):
        rows = slice(r * hb, (r + 1) * hb)
        gate = jax.nn.sigmoid(jnp.dot(h_ref[rows, :], wg_ref[...], preferred_element_type=F32))
        emb = jnp.dot(p_ref[rows, :].astype(BF16), wp_ref[...], preferred_element_type=F32)
        y = x_ref[rows, :] + gate * emb
        if last:
            y = (y * lax.rsqrt(jnp.mean(y * y, axis=-1, keepdims=True) + EPS)) * fgain_ref[...]
        out_ref[rows, :] = y


def _ple(xf, gain, wg, p3, layer, wp, final_gain, last, tm):
    m, d = xf.shape
    pd = p3.shape[2]
    return pl.pallas_call(
        functools.partial(_ple_kernel, last),
        grid=(m // tm,),
        in_specs=[
            pl.BlockSpec((tm, d), lambda i: (i, 0)),
            pl.BlockSpec((1, d), lambda i: (0, 0)),
            _resident((None, d, d), lambda i: (layer, 0, 0), 1),
            pl.BlockSpec((None, tm, pd), lambda i: (layer, i, 0)),
            _resident((None, pd, d), lambda i: (layer, 0, 0), 1),
            pl.BlockSpec((1, d), lambda i: (0, 0)),
        ],
        out_specs=pl.BlockSpec((tm, d), lambda i: (i, 0)),
        out_shape=jax.ShapeDtypeStruct((m, d), F32),
        scratch_shapes=[pltpu.VMEM((tm, d), BF16)],
        compiler_params=_params(("parallel",)),
        name="ple",
    )(xf, gain.reshape(1, d), wg, p3, wp, final_gain.reshape(1, d))


def kernel(x, p, norm_mix, w_in, conv_qkv, a_log, dt_bias, head_norm, sgu_norm, w_spatial, b_spatial, w_branch_a, w_branch_b, w_out, norm_ffn, w_ffn_up, conv_ffn, b_conv_ffn, w_ffn_down, norm_ple, w_ple_gate, w_ple_proj, norm_final):
    batch, seq, d = x.shape
    depth = w_in.shape[0]
    m = batch * seq
    assert d == V_WIDTH and seq % CHUNK_B == 0 and w_in.shape[2] == N_MAIN + N_NARROW
    tm = _row_tile(seq, 1024)
    tm_res = _row_tile(seq, 512)
    xf = x.reshape(m, d).astype(F32)
    p3 = p.reshape(depth, m, p.shape[-1])
    wa_b, wb_b, wo_b = w_branch_a.astype(BF16), w_branch_b.astype(BF16), w_out.astype(BF16)
    wu_b, wd_b = w_ffn_up.astype(BF16), w_ffn_down.astype(BF16)
    wg_b, wp_b = w_ple_gate.astype(BF16), w_ple_proj.astype(BF16)
    wi_b = jnp.swapaxes(w_in, 1, 2).astype(BF16)
    for i in range(depth):
        proj, nar = _inproj(xf, norm_mix[i], wi_b, i, tm, _col_tile(COL_NARROW, 1024))
        o = _delta(proj, nar, conv_qkv[i].astype(F32), a_log[i], dt_bias[i], head_norm[i].astype(F32), batch, seq)
        s = _sgu(proj, sgu_norm[i].astype(F32), w_spatial[i].astype(F32), b_spatial[i], _row_tile(seq, 512))
        merged = _merge(o, s, proj, wa_b, wb_b, i, tm, 512)
        xf = _mm_res(merged, wo_b, i, xf, tm_res, d, "out_proj")
        act = _ffn_up(xf, norm_ffn[i].astype(F32), wu_b, i, conv_ffn[i].astype(F32), b_conv_ffn[i], seq, tm,
                      _col_tile(D_FF, 512))
        xf = _mm_res(act, wd_b, i, xf, tm, 512, "ffn_down")
        xf = _ple(xf, norm_ple[i].astype(F32), wg_b, p3, i, wp_b, norm_final.astype(F32), i == depth - 1, tm_res)
    return xf.reshape(batch, seq, d).astype(x.dtype)
```
